```python
import jax, jax.numpy as jnp
from jax import lax
import numpy as np

D_MODEL = 1024
BATCH = 8
SEQ = 2048
DEPTH = 4

GRID_W = 64
N_MEM = 256
HEAD_DIM = 64
N_BRANCH = 4
BRANCH_WIDTH = D_MODEL // N_BRANCH
IN_WIDTH = 9 * BRANCH_WIDTH
RET_HEADS = BRANCH_WIDTH // HEAD_DIM
RET_CHUNK = 128
ROPE_THETA = 10000.0
POOL_WINDOWS = (2, 4, 8, 16)
POOL_GROUPS = len(POOL_WINDOWS)
POOL_GROUP_DIM = BRANCH_WIDTH // POOL_GROUPS
NA_HEADS = BRANCH_WIDTH // HEAD_DIM
NA_WIN_ROWS = 8
NA_WIN_COLS = 16
NA_QBLOCK_COLS = 16
NA_KBLOCK_COLS = 2 * NA_QBLOCK_COLS
MEM_HEADS = BRANCH_WIDTH // HEAD_DIM
FF_HIDDEN = -(-8 * D_MODEL // (3 * 256)) * 256
NEG_INF = -1e30
EPS = 1e-6

kernel_name = "hybrid_retention_pool_natten_memory_encoder"


def rms_norm(x, g):
    xf = x.astype(jnp.float32)
    y = xf * lax.rsqrt(jnp.mean(xf * xf, axis=-1, keepdims=True) + EPS)
    return (y * g.astype(jnp.float32)).astype(x.dtype)


def split_heads(t, n_heads):
    b, s, _ = t.shape
    return t.reshape(b, s, n_heads, -1).transpose(0, 2, 1, 3)


def merge_heads(t):
    b, h, s, d = t.shape
    return t.transpose(0, 2, 1, 3).reshape(b, s, h * d)


def rotary(t, pos):
    half = t.shape[-1] // 2
    inv = ROPE_THETA ** (-jnp.arange(half, dtype=jnp.float32) / half)
    ang = pos[:, None] * inv[None, :]
    cos, sin = jnp.cos(ang), jnp.sin(ang)
    tf = t.astype(jnp.float32)
    t1, t2 = tf[..., :half], tf[..., half:]
    return jnp.concatenate([t1 * cos - t2 * sin, t1 * sin + t2 * cos], axis=-1).astype(t.dtype)


def retention_dir(q, k, v, log_gamma, include_diag):
    b, h, s, d = q.shape
    c = RET_CHUNK
    n = s // c
    dt = q.dtype
    qc, kc, vc = (t.reshape(b, h, n, c, d) for t in (q, k, v))
    idx = jnp.arange(c, dtype=jnp.float32)
    diff = idx[:, None] - idx[None, :]
    mask = (diff >= 0) if include_diag else (diff > 0)
    lg = log_gamma.astype(jnp.float32)[:, None]
    d_intra = jnp.where(mask[None], jnp.exp(jnp.where(mask, diff, 0.0)[None] * lg[:, :, None]), 0.0)
    scores = jnp.einsum('bhncd,bhnmd->bhncm', qc, kc) * d_intra[None, :, None].astype(dt)
    intra = jnp.einsum('bhncm,bhnme->bhnce', scores, vc)
    k_decay = jnp.exp((c - 1 - idx)[None, :] * lg).astype(dt)
    kv = jnp.einsum('bhncd,bhnce->nbhde', kc * k_decay[None, :, None, :, None], vc)
    chunk_decay = jnp.exp(c * lg[:, 0]).astype(dt)[None, :, None, None]

    def step(state, kv_n):
        return chunk_decay * state + kv_n, state

    _, states = lax.scan(step, jnp.zeros_like(kv[0]), kv)
    q_decay = jnp.exp((idx + 1)[None, :] * lg).astype(dt)
    cross = jnp.einsum('bhncd,nbhde->bhnce', qc * q_decay[None, :, None, :, None], states)
    return (intra + cross).reshape(b, h, s, d)


def bidirectional_retention(q, k, v, log_gamma_fwd, log_gamma_bwd):
    fwd = retention_dir(q, k, v, log_gamma_fwd, True)
    flip = lambda t: jnp.flip(t, axis=2)
    bwd = flip(retention_dir(flip(q), flip(k), flip(v), log_gamma_bwd, False))
    return fwd + bwd


def multiscale_pool(v, w_group, scale):
    b, s, cw = v.shape
    vg = v.reshape(b, s, POOL_GROUPS, POOL_GROUP_DIM)
    cs = jnp.cumsum(vg.astype(jnp.float32), axis=1)
    cs = jnp.concatenate([jnp.zeros_like(cs[:, :1]), cs], axis=1)
    t = np.arange(s)[:, None]
    half = np.array(POOL_WINDOWS)[None, :] // 2
    lo = np.clip(t - half, 0, s)
    hi = np.clip(t + half, 0, s)
    g_idx = np.arange(POOL_GROUPS)[None, :]
    win_sum = cs[:, hi, g_idx] - cs[:, lo, g_idx]
    count = jnp.asarray((hi - lo)[None, :, :, None], dtype=jnp.float32)
    pooled = (win_sum / count).astype(v.dtype) - vg
    mixed = jnp.einsum('bsgc,gce->bsge', pooled, w_group)
    return mixed.reshape(b, s, cw) * scale


def neighbourhood_attention(q, k, v, rpb):
    b, h, s, d = q.shape
    rows = s // GRID_W
    wr = min(NA_WIN_ROWS, rows)
    n_cb = GRID_W // NA_QBLOCK_COLS
    r = np.arange(rows)
    row_idx = np.clip(r - wr // 2, 0, rows - wr)[:, None] + np.arange(wr)[None, :]
    cb = np.arange(n_cb)
    kcol_idx = np.clip(cb * NA_QBLOCK_COLS - NA_WIN_COLS // 2, 0, GRID_W - NA_KBLOCK_COLS)[:, None] \
        + np.arange(NA_KBLOCK_COLS)[None, :]
    qcol = cb[:, None] * NA_QBLOCK_COLS + np.arange(NA_QBLOCK_COLS)[None, :]
    qwin = np.clip(qcol - NA_WIN_COLS // 2, 0, GRID_W - NA_WIN_COLS)
    col_mask = (kcol_idx[:, None, :] >= qwin[:, :, None]) & (kcol_idx[:, None, :] < qwin[:, :, None] + NA_WIN_COLS)
    row_off = row_idx - r[:, None]
    col_off = np.clip(kcol_idx[:, None, :] - qcol[:, :, None], -(NA_WIN_COLS - 1), NA_WIN_COLS - 1)
    bias = rpb[:, row_off[:, None, None, :, None] + NA_WIN_ROWS - 1,
               col_off[None, :, :, None, :] + NA_WIN_COLS - 1]
    bias = jnp.where(col_mask[None, None, :, :, None, :], bias.astype(jnp.float32), NEG_INF)

    qg = q.reshape(b, h, rows, n_cb, NA_QBLOCK_COLS, d)
    k_grid = k.reshape(b, h, rows, GRID_W, d)
    v_grid = v.reshape(b, h, rows, GRID_W, d)
    ri = row_idx[:, None, :, None]
    ci = kcol_idx[None, :, None, :]
    kg = k_grid[:, :, ri, ci]
    vg = v_grid[:, :, ri, ci]
    sc = jnp.einsum('bhrnqd,bhrnwkd->bhrnqwk', qg, kg).astype(jnp.float32) * (d ** -0.5) + bias[None]
    p = jax.nn.softmax(sc, axis=(-2, -1))
    o = jnp.einsum('bhrnqwk,bhrnwkd->bhrnqd', p.astype(v.dtype), vg)
    return o.reshape(b, h, s, d)


def memory_attention(q, mk, mv):
    sc = jnp.einsum('bhsd,bhmd->bhsm', q, mk).astype(jnp.float32) * (q.shape[-1] ** -0.5)
    p = jax.nn.softmax(sc, axis=-1)
    return jnp.einsum('bhsm,bhmd->bhsd', p.astype(mv.dtype), mv)


def hybrid_layer(x, mem, norm_mix_g, norm_mem_g, w_in, w_gate, ret_decay_fwd, ret_decay_bwd,
                 ret_norm_g, pool_w, pool_scale, na_q_norm_g, na_k_norm_g, na_rpb,
                 mem_q_norm_g, mem_k_norm_g, w_mem_kv, w_branch, w_out, norm_ffn_g,
                 w_ffn_in, w_ffn_out):
    b, s, dm = x.shape
    h = rms_norm(x, norm_mix_g)
    proj = h @ w_in
    rq, rk, rv, rg, pv, nq, nk, nv, mq = jnp.split(proj, 9, axis=-1)

    pos = jnp.arange(s, dtype=jnp.float32)
    rq_h = rotary(split_heads(rq, RET_HEADS), pos) * (HEAD_DIM ** -0.5)
    rk_h = rotary(split_heads(rk, RET_HEADS), pos)
    ret = bidirectional_retention(rq_h, rk_h, split_heads(rv, RET_HEADS),
                                  jax.nn.log_sigmoid(ret_decay_fwd.astype(jnp.float32)),
                                  jax.nn.log_sigmoid(ret_decay_bwd.astype(jnp.float32)))
    ret = merge_heads(rms_norm(ret, ret_norm_g.reshape(RET_HEADS, 1, HEAD_DIM))) * jax.nn.silu(rg)

    pool = multiscale_pool(pv, pool_w, pool_scale)

    na = merge_heads(neighbourhood_attention(rms_norm(split_heads(nq, NA_HEADS), na_q_norm_g),
                                             rms_norm(split_heads(nk, NA_HEADS), na_k_norm_g),
                                             split_heads(nv, NA_HEADS), na_rpb))

    mk, mv = jnp.split(rms_norm(mem, norm_mem_g) @ w_mem_kv, 2, axis=-1)
    mo = merge_heads(memory_attention(rms_norm(split_heads(mq, MEM_HEADS), mem_q_norm_g),
                                      rms_norm(split_heads(mk, MEM_HEADS), mem_k_norm_g),
                                      split_heads(mv, MEM_HEADS)))

    branches = jnp.stack([ret, pool, na, mo], axis=2)
    up = jnp.einsum('bsnc,ncd->bsnd', branches, w_branch)
    gates = jax.nn.sigmoid(h @ w_gate).reshape(b, s, N_BRANCH, dm)
    merged = jnp.einsum('bsnd,bsnd->bsd', gates, up)
    x = x + merged @ w_out

    a, g = jnp.split(rms_norm(x, norm_ffn_g) @ w_ffn_in, 2, axis=-1)
    return x + (jax.nn.silu(a) * g) @ w_ffn_out


def setup_inputs(seed: int = 0) -> dict:
    key = jax.random.key(seed)
    ks = jax.random.split(key, 22)
    f32 = jnp.float32
    L, D, BW = DEPTH, D_MODEL, BRANCH_WIDTH

    def nrm(k, shape, scale):
        return jax.random.normal(k, shape, f32) * scale

    base_logit = jnp.log(2.0 ** (5.0 + jnp.arange(RET_HEADS, dtype=f32)) - 1.0)
    return {
        "x": nrm(ks[0], (BATCH, SEQ, D), 1.0),
        "mem": nrm(ks[1], (BATCH, N_MEM, D), 1.0),
        "norm_mix_g": 1.0 + nrm(ks[2], (L, D), 0.02),
        "norm_mem_g": 1.0 + nrm(ks[3], (L, D), 0.02),
        "w_in": nrm(ks[4], (L, D, IN_WIDTH), D ** -0.5),
        "w_gate": nrm(ks[5], (L, D, N_BRANCH * D), D ** -0.5),
        "ret_decay_fwd": base_logit[None, :] + nrm(ks[6], (L, RET_HEADS), 0.1),
        "ret_decay_bwd": base_logit[None, :] + nrm(ks[7], (L, RET_HEADS), 0.1),
        "ret_norm_g": 1.0 + nrm(ks[8], (L, BW), 0.02),
        "pool_w": nrm(ks[9], (L, POOL_GROUPS, POOL_GROUP_DIM, POOL_GROUP_DIM), POOL_GROUP_DIM ** -0.5),
        "pool_scale": 1.0 + nrm(ks[10], (L, BW), 0.02),
        "na_q_norm_g": 1.0 + nrm(ks[11], (L, HEAD_DIM), 0.02),
        "na_k_norm_g": 1.0 + nrm(ks[12], (L, HEAD_DIM), 0.02),
        "na_rpb": nrm(ks[13], (L, NA_HEADS, 2 * NA_WIN_ROWS - 1, 2 * NA_WIN_COLS - 1), 0.02),
        "mem_q_norm_g": 1.0 + nrm(ks[14], (L, HEAD_DIM), 0.02),
        "mem_k_norm_g": 1.0 + nrm(ks[15], (L, HEAD_DIM), 0.02),
        "w_mem_kv": nrm(ks[16], (L, D, 2 * BW), D ** -0.5),
        "w_branch": nrm(ks[17], (L, N_BRANCH, BW, D), BW ** -0.5),
        "w_out": nrm(ks[18], (L, D, D), D ** -0.5),
        "norm_ffn_g": 1.0 + nrm(ks[19], (L, D), 0.02),
        "w_ffn_in": nrm(ks[20], (L, D, 2 * FF_HIDDEN), D ** -0.5),
        "w_ffn_out": nrm(ks[21], (L, FF_HIDDEN, D), FF_HIDDEN ** -0.5),
    }


def reference(x, mem, norm_mix_g, norm_mem_g, w_in, w_gate, ret_decay_fwd, ret_decay_bwd,
              ret_norm_g, pool_w, pool_scale, na_q_norm_g, na_k_norm_g, na_rpb,
              mem_q_norm_g, mem_k_norm_g, w_mem_kv, w_branch, w_out, norm_ffn_g,
              w_ffn_in, w_ffn_out):
    for l in range(DEPTH):
        x = hybrid_layer(x, mem, norm_mix_g[l], norm_mem_g[l], w_in[l], w_gate[l],
                         ret_decay_fwd[l], ret_decay_bwd[l], ret_norm_g[l], pool_w[l],
                         pool_scale[l], na_q_norm_g[l], na_k_norm_g[l], na_rpb[l],
                         mem_q_norm_g[l], mem_k_norm_g[l], w_mem_kv[l], w_branch[l],
                         w_out[l], norm_ffn_g[l], w_ffn_in[l], w_ffn_out[l])
    return x
```

```python
import functools

import numpy as np
import jax
import jax.numpy as jnp
from jax import lax
from jax.experimental import pallas as pl
from jax.experimental.pallas import tpu as pltpu

F32 = jnp.float32
BF16 = jnp.bfloat16

HEAD_DIM = 64
BRANCH_WIDTH = 256
N_HEADS = BRANCH_WIDTH // HEAD_DIM
N_BRANCH = 4
GRID_W = 64
RET_CHUNK = 128
ROPE_THETA = 10000.0
POOL_WINDOWS = (2, 4, 8, 16)
POOL_PAD = 16
NA_WIN_ROWS = 8
NA_WIN_COLS = 16
NEG_INF = -1e30
EPS = 1e-6
QK_SCALE = HEAD_DIM ** -0.5

VMEM_LIMIT_BYTES = 56 * 1024 * 1024


def _cparams(n_grid_dims):
    return pltpu.CompilerParams(
        dimension_semantics=("arbitrary",) * n_grid_dims,
        vmem_limit_bytes=VMEM_LIMIT_BYTES,
    )


def _resident(block_shape, index_map):
    return pl.BlockSpec(block_shape, index_map, pipeline_mode=pl.Buffered(1))


def _rms_rows(x, g):
    ms = jnp.mean(x * x, axis=-1, keepdims=True)
    return x * lax.rsqrt(ms + EPS) * g


def _head_id(shape, axis):
    return lax.shift_right_logical(lax.broadcasted_iota(jnp.int32, shape, axis), 6)


def _head_block_ones():
    shape = (BRANCH_WIDTH, BRANCH_WIDTH)
    return (_head_id(shape, 0) == _head_id(shape, 1)).astype(BF16)


def _head_sumsq(t, ones_bd):
    t2 = t * t
    hi = t2.astype(BF16)
    lo = (t2 - hi.astype(F32)).astype(BF16)
    return (jnp.dot(hi, ones_bd, preferred_element_type=F32)
            + jnp.dot(lo, ones_bd, preferred_element_type=F32))


def _head_rms(t, g, ones_bd):
    ms = _head_sumsq(t, ones_bd) * (1.0 / HEAD_DIM)
    return t * lax.rsqrt(ms + EPS) * g


def _log_sigmoid(x):
    return jnp.minimum(x, 0.0) - jnp.log1p(jnp.exp(-jnp.abs(x)))


def _dot_nt(a, b):
    return lax.dot_general(a, b, (((1,), (1,)), ((), ())), preferred_element_type=F32)


def _dot_tn(a, b):
    return lax.dot_general(a, b, (((0,), (0,)), ((), ())), preferred_element_type=F32)


def _inproj_kernel(x_ref, g_ref, w_ref, o_ref):
    h = _rms_rows(x_ref[...], g_ref[...]).astype(BF16)
    o_ref[...] = jnp.dot(h, w_ref[...], preferred_element_type=F32).astype(o_ref.dtype)


def _inproj(x2, norm_g, w_in, layer, tm):
    t, d = x2.shape
    n_out = w_in.shape[-1]
    return pl.pallas_call(
        _inproj_kernel,
        grid=(t // tm,),
        in_specs=[
            pl.BlockSpec((tm, d), lambda i: (i, 0)),
            _resident((None, 1, d), lambda i: (layer, 0, 0)),
            _resident((None, d, n_out), lambda i: (layer, 0, 0)),
        ],
        out_specs=pl.BlockSpec((tm, n_out), lambda i: (i, 0)),
        out_shape=jax.ShapeDtypeStruct((t, n_out), BF16),
        compiler_params=_cparams(1),
        name="inproj",
    )(x2, norm_g, w_in)


def _ret_kernel(q_ref, k_ref, v_ref, g_ref, cos_ref, sin_ref, dec_ref, gn_ref, o_ref,
                qs_ref, ks_ref, sf_ref, sb_ref, st_ref, dm_ref):
    c = RET_CHUNK
    n_chunks = q_ref.shape[0] // c
    half = BRANCH_WIDTH // 2

    lane = lax.broadcasted_iota(jnp.int32, (1, BRANCH_WIDTH), 1)
    head_qk = lax.shift_right_logical(jnp.bitwise_and(lane, half - 1), 5)
    head_v = lax.shift_right_logical(lane, 6)

    lg = _log_sigmoid(dec_ref[...])
    lg2 = jnp.concatenate([lg, lg], axis=1)

    def per_lane(head_vec, row0):
        out = jnp.zeros((1, BRANCH_WIDTH), F32)
        for h in range(N_HEADS):
            out = jnp.where(head_vec == h, lg2[row0 + h:row0 + h + 1, :], out)
        return out

    lgf_qk, lgb_qk = per_lane(head_qk, 0), per_lane(head_qk, N_HEADS)
    lgf_v, lgb_v = per_lane(head_v, 0), per_lane(head_v, N_HEADS)

    idx = lax.broadcasted_iota(jnp.int32, (c, 1), 0).astype(F32)
    kdec_f = jnp.exp((c - 1 - idx) * lgf_qk)
    qdec_f = jnp.exp((idx + 1) * lgf_qk)
    kdec_b = jnp.exp(idx * lgb_qk)
    qdec_b = jnp.exp((c - idx) * lgb_qk)
    cdec_f = jnp.exp(c * lgf_v)
    cdec_b = jnp.exp(c * lgb_v)

    diff = (lax.broadcasted_iota(jnp.int32, (c, c), 0)
            - lax.broadcasted_iota(jnp.int32, (c, c), 1)).astype(F32)
    for h in range(N_HEADS):
        lf = lg[h:h + 1, :]
        lb = lg[N_HEADS + h:N_HEADS + h + 1, :]
        dm_ref[h] = jnp.where(diff >= 0, jnp.exp(jnp.maximum(diff, 0.0) * lf),
                              jnp.exp(jnp.maximum(-diff, 0.0) * lb))

    row_head = lax.shift_right_logical(
        jnp.bitwise_and(lax.broadcasted_iota(jnp.int32, (BRANCH_WIDTH, BRANCH_WIDTH), 0), half - 1), 5)
    col_head = _head_id((BRANCH_WIDTH, BRANCH_WIDTH), 1)
    same_head = row_head == col_head

    def rotate(t, cos, sin, scale):
        a, b = t[:, :half], t[:, half:]
        return jnp.concatenate([a * cos - b * sin, a * sin + b * cos], axis=1) * scale

    st_ref[...] = jnp.zeros_like(st_ref)

    def fwd_body(j, carry):
        rows = pl.ds(pl.multiple_of(j * c, c), c)
        cos, sin = cos_ref[rows, :], sin_ref[rows, :]
        q = rotate(q_ref[rows, :].astype(F32), cos, sin, QK_SCALE)
        k = rotate(k_ref[rows, :].astype(F32), cos, sin, 1.0)
        qs_ref[rows, :] = q
        ks_ref[rows, :] = k
        kv = _dot_tn((k * kdec_f).astype(BF16), v_ref[rows, :])
        state = st_ref[...]
        sf_ref[j] = state.astype(BF16)
        st_ref[...] = cdec_f * state + jnp.where(same_head, kv, 0.0)
        return carry

    lax.fori_loop(0, n_chunks, fwd_body, 0)

    st_ref[...] = jnp.zeros_like(st_ref)

    def bwd_body(i, carry):
        j = n_chunks - 1 - i
        rows = pl.ds(pl.multiple_of(j * c, c), c)
        kv = _dot_tn((ks_ref[rows, :] * kdec_b).astype(BF16), v_ref[rows, :])
        state = st_ref[...]
        sb_ref[j] = state.astype(BF16)
        st_ref[...] = cdec_b * state + jnp.where(same_head, kv, 0.0)
        return carry

    lax.fori_loop(0, n_chunks, bwd_body, 0)

    ones_bd = _head_block_ones()
    gn = gn_ref[...]

    def out_body(j, carry):
        rows = pl.ds(pl.multiple_of(j * c, c), c)
        q = qs_ref[rows, :]
        kb = ks_ref[rows, :].astype(BF16)
        v = v_ref[rows, :]
        acc = jnp.dot((q * qdec_f).astype(BF16), sf_ref[j], preferred_element_type=F32)
        acc = acc + jnp.dot((q * qdec_b).astype(BF16), sb_ref[j], preferred_element_type=F32)
        for h in range(N_HEADS):
            qm = jnp.where(head_qk == h, q, 0.0).astype(BF16)
            s = _dot_nt(qm, kb) * dm_ref[h]
            pv = jnp.dot(s.astype(BF16), v, preferred_element_type=F32)
            acc = acc + jnp.where(head_v == h, pv, 0.0)
        y = _head_rms(acc, gn, ones_bd)
        gate = g_ref[rows, :].astype(F32)
        o_ref[rows, :] = (y * (gate * jax.nn.sigmoid(gate))).astype(o_ref.dtype)
        return carry

    lax.fori_loop(0, n_chunks, out_body, 0)


def _retention(proj, cos, sin, dec, gn, layer):
    b, s, _ = proj.shape
    bw = BRANCH_WIDTH
    n_chunks = s // RET_CHUNK
    col = lambda cidx: pl.BlockSpec((None, s, bw), lambda i: (i, 0, cidx))
    return pl.pallas_call(
        _ret_kernel,
        grid=(b,),
        in_specs=[
            col(0), col(1), col(2), col(3),
            _resident((s, bw // 2), lambda i: (0, 0)),
            _resident((s, bw // 2), lambda i: (0, 0)),
            _resident((None, 2 * N_HEADS, 128), lambda i: (layer, 0, 0)),
            _resident((None, 1, bw), lambda i: (layer, 0, 0)),
        ],
        out_specs=pl.BlockSpec((None, s, bw), lambda i: (i, 0, 0)),
        out_shape=jax.ShapeDtypeStruct((b, s, bw), BF16),
        scratch_shapes=[
            pltpu.VMEM((s, bw), F32),
            pltpu.VMEM((s, bw), F32),
            pltpu.VMEM((n_chunks, bw, bw), BF16),
            pltpu.VMEM((n_chunks, bw, bw), BF16),
            pltpu.VMEM((bw, bw), F32),
            pltpu.VMEM((N_HEADS, RET_CHUNK, RET_CHUNK), F32),
        ],
        compiler_params=_cparams(1),
        name="retention",
    )(proj, proj, proj, proj, cos, sin, dec, gn)


def _pool_kernel(v_ref, w_ref, sc_ref, o_ref, pad_ref):
    s = v_ref.shape[0]
    pad = POOL_PAD
    tile = 512
    ext = tile + 2 * pad

    pad_ref[0:pad, :] = jnp.zeros((pad, BRANCH_WIDTH), F32)
    pad_ref[pad + s:pad + s + pad, :] = jnp.zeros((pad, BRANCH_WIDTH), F32)
    pad_ref[pad:pad + s, :] = v_ref[...].astype(F32)

    grp = _head_id((1, BRANCH_WIDTH), 1)
    halfw = jnp.where(grp == 0, POOL_WINDOWS[0] // 2,
                      jnp.where(grp == 1, POOL_WINDOWS[1] // 2,
                                jnp.where(grp == 2, POOL_WINDOWS[2] // 2, POOL_WINDOWS[3] // 2)))

    for ci in range(s // tile):
        p = pad_ref[ci * tile:ci * tile + ext, :]
        s2 = p + pltpu.roll(p, 1, 0)
        s4 = pltpu.roll(s2, 1, 0) + pltpu.roll(s2, ext - 1, 0)
        s8 = pltpu.roll(s4, 2, 0) + pltpu.roll(s4, ext - 2, 0)
        s16 = pltpu.roll(s8, 4, 0) + pltpu.roll(s8, ext - 4, 0)
        win = jnp.where(grp == 0, s2, jnp.where(grp == 1, s4, jnp.where(grp == 2, s8, s16)))
        win = win[pad:pad + tile, :]
        t = ci * tile + lax.broadcasted_iota(jnp.int32, (tile, 1), 0)
        count = (jnp.minimum(t + halfw, s) - jnp.maximum(t - halfw, 0)).astype(F32)
        pooled = win / count - p[pad:pad + tile, :]
        mixed = jnp.dot(pooled.astype(BF16), w_ref[...], preferred_element_type=F32) * sc_ref[...]
        o_ref[ci * tile:(ci + 1) * tile, :] = mixed.astype(o_ref.dtype)


def _pooling(proj, w_bd, scale, layer):
    b, s, _ = proj.shape
    bw = BRANCH_WIDTH
    return pl.pallas_call(
        _pool_kernel,
        grid=(b,),
        in_specs=[
            pl.BlockSpec((None, s, bw), lambda i: (i, 0, 4)),
            _resident((None, bw, bw), lambda i: (layer, 0, 0)),
            _resident((None, 1, bw), lambda i: (layer, 0, 0)),
        ],
        out_specs=pl.BlockSpec((None, s, bw), lambda i: (i, 0, 0)),
        out_shape=jax.ShapeDtypeStruct((b, s, bw), BF16),
        scratch_shapes=[pltpu.VMEM((s + 2 * POOL_PAD, bw), F32)],
        compiler_params=_cparams(1),
        name="pooling",
    )(proj, w_bd, scale)


def _stack_heads(t, head_vec):
    return jnp.concatenate([jnp.where(head_vec == h, t, jnp.zeros_like(t)) for h in range(N_HEADS)], axis=0)


def _unstack_heads(t, head_vec, n):
    out = jnp.where(head_vec == 0, t[0:n], 0.0)
    for h in range(1, N_HEADS):
        out = out + jnp.where(head_vec == h, t[h * n:(h + 1) * n], 0.0)
    return out


def _na_kernel(q_ref, k_ref, v_ref, gq_ref, gk_ref, bias_ref, o_ref, qn_ref, kn_ref):
    s = q_ref.shape[0]
    rows = s // GRID_W
    wr = NA_WIN_ROWS
    ones_bd = _head_block_ones()
    head_v = _head_id((1, BRANCH_WIDTH), 1)

    tile = 512
    for ci in range(s // tile):
        sl = slice(ci * tile, (ci + 1) * tile)
        qn_ref[sl, :] = (_head_rms(q_ref[sl, :].astype(F32), gq_ref[...], ones_bd) * QK_SCALE).astype(BF16)
        kn_ref[sl, :] = _head_rms(k_ref[sl, :].astype(F32), gk_ref[...], ones_bd).astype(BF16)

    def row_body(r, carry):
        r0 = jnp.clip(r - wr // 2, 0, rows - wr)
        pat = jnp.where(r < wr // 2, r, jnp.where(r > rows - wr // 2, r - (rows - wr), wr // 2))
        qrows = pl.ds(pl.multiple_of(r * GRID_W, GRID_W), GRID_W)
        krows = pl.ds(pl.multiple_of(r0 * GRID_W, GRID_W), wr * GRID_W)
        qst = _stack_heads(qn_ref[qrows, :], head_v)
        sc = _dot_nt(qst, kn_ref[krows, :]) + bias_ref[pat]
        m = jnp.max(sc, axis=-1, keepdims=True)
        e = jnp.exp(sc - m)
        l = jnp.sum(e, axis=-1, keepdims=True)
        pv = jnp.dot(e.astype(BF16), v_ref[krows, :], preferred_element_type=F32) / l
        o_ref[qrows, :] = _unstack_heads(pv, head_v, GRID_W).astype(o_ref.dtype)
        return carry

    lax.fori_loop(0, rows, row_body, 0)


def _neighbourhood(proj, gq, gk, bias_pat, layer):
    b, s, _ = proj.shape
    bw = BRANCH_WIDTH
    n_pat, mq, nk = bias_pat.shape[1:]
    col = lambda cidx: pl.BlockSpec((None, s, bw), lambda i: (i, 0, cidx))
    return pl.pallas_call(
        _na_kernel,
        grid=(b,),
        in_specs=[
            col(5), col(6), col(7),
            _resident((None, 1, bw), lambda i: (layer, 0, 0)),
            _resident((None, 1, bw), lambda i: (layer, 0, 0)),
            _resident((None, n_pat, mq, nk), lambda i: (layer, 0, 0, 0)),
        ],
        out_specs=pl.BlockSpec((None, s, bw), lambda i: (i, 0, 0)),
        out_shape=jax.ShapeDtypeStruct((b, s, bw), BF16),
        scratch_shapes=[pltpu.VMEM((s, bw), BF16), pltpu.VMEM((s, bw), BF16)],
        compiler_params=_cparams(1),
        name="neighbourhood",
    )(proj, proj, proj, gq, gk, bias_pat)


def _na_bias_patterns(rpb, rows):
    wr, wc = NA_WIN_ROWS, NA_WIN_COLS
    rep_rows = np.array(list(range(wr // 2)) + [wr // 2] + list(range(rows - wr // 2 + 1, rows)))
    row_start = np.clip(rep_rows - wr // 2, 0, rows - wr)
    row_off = row_start[:, None] + np.arange(wr)[None, :] - rep_rows[:, None]
    qc = np.arange(GRID_W)[:, None]
    kc = np.arange(GRID_W)[None, :]
    qwin = np.clip(qc - wc // 2, 0, GRID_W - wc)
    col_mask = (kc >= qwin) & (kc < qwin + wc)
    col_off = np.clip(kc - qc, -(wc - 1), wc - 1) + wc - 1
    gathered = rpb[:, :, (row_off + wr - 1)[:, None, :, None], col_off[None, :, None, :]]
    gathered = jnp.where(col_mask[None, None, None, :, None, :], gathered.astype(F32), NEG_INF)
    n_layers, n_heads, n_pat = gathered.shape[:3]
    gathered = gathered.transpose(0, 2, 1, 3, 4, 5)
    return gathered.reshape(n_layers, n_pat, n_heads * GRID_W, wr * GRID_W)


def _memkv_kernel(m_ref, g_ref, w_ref, gk_ref, k_ref, v_ref):
    mn = _rms_rows(m_ref[...], g_ref[...]).astype(BF16)
    kv = jnp.dot(mn, w_ref[...], preferred_element_type=F32)
    k_ref[...] = _head_rms(kv[:, :BRANCH_WIDTH], gk_ref[...], _head_block_ones()).astype(k_ref.dtype)
    v_ref[...] = kv[:, BRANCH_WIDTH:].astype(v_ref.dtype)


def _memkv(mem, norm_g, w_kv, gk):
    b, m, d = mem.shape
    n_layers = w_kv.shape[0]
    bw = BRANCH_WIDTH
    out = jax.ShapeDtypeStruct((n_layers, b, m, bw), BF16)
    return pl.pallas_call(
        _memkv_kernel,
        grid=(n_layers, b),
        in_specs=[
            pl.BlockSpec((None, m, d), lambda l, i: (i, 0, 0)),
            pl.BlockSpec((None, 1, d), lambda l, i: (l, 0, 0)),
            pl.BlockSpec((None, d, 2 * bw), lambda l, i: (l, 0, 0)),
            pl.BlockSpec((None, 1, bw), lambda l, i: (l, 0, 0)),
        ],
        out_specs=[pl.BlockSpec((None, None, m, bw), lambda l, i: (l, i, 0, 0))] * 2,
        out_shape=[out, out],
        compiler_params=_cparams(2),
        name="memkv",
    )(mem, norm_g, w_kv, gk)


def _memattn_kernel(q_ref, gq_ref, k_ref, v_ref, o_ref):
    tq = q_ref.shape[0]
    sub = 128
    head_v = _head_id((1, BRANCH_WIDTH), 1)
    qn = (_head_rms(q_ref[...].astype(F32), gq_ref[...], _head_block_ones()) * QK_SCALE).astype(BF16)
    mk = k_ref[...]
    mv = v_ref[...]
    for ti in range(tq // sub):
        qst = _stack_heads(qn[ti * sub:(ti + 1) * sub, :], head_v)
        sc = _dot_nt(qst, mk)
        m = jnp.max(sc, axis=-1, keepdims=True)
        e = jnp.exp(sc - m)
        l = jnp.sum(e, axis=-1, keepdims=True)
        pv = jnp.dot(e.astype(BF16), mv, preferred_element_type=F32) / l
        o_ref[ti * sub:(ti + 1) * sub, :] = _unstack_heads(pv, head_v, sub).astype(o_ref.dtype)


def _memattn(proj, gq, mk, mv, layer, tq):
    b, s, _ = proj.shape
    bw = BRANCH_WIDTH
    m = mk.shape[2]
    return pl.pallas_call(
        _memattn_kernel,
        grid=(b, s // tq),
        in_specs=[
            pl.BlockSpec((None, tq, bw), lambda i, j: (i, j, 8)),
            _resident((None, 1, bw), lambda i, j: (layer, 0, 0)),
            pl.BlockSpec((None, None, m, bw), lambda i, j: (layer, i, 0, 0)),
            pl.BlockSpec((None, None, m, bw), lambda i, j: (layer, i, 0, 0)),
        ],
        out_specs=pl.BlockSpec((None, tq, bw), lambda i, j: (i, j, 0)),
        out_shape=jax.ShapeDtypeStruct((b, s, bw), BF16),
        compiler_params=_cparams(2),
        name="memattn",
    )(proj, gq, mk, mv)


def _merge_kernel(x_ref, g_ref, b0_ref, b1_ref, b2_ref, b3_ref, wg_ref, wb_ref, wo_ref, o_ref):
    x = x_ref[...]
    d = x.shape[-1]
    h = _rms_rows(x, g_ref[...]).astype(BF16)
    merged = None
    for n, br_ref in enumerate((b0_ref, b1_ref, b2_ref, b3_ref)):
        gate = jax.nn.sigmoid(jnp.dot(h, wg_ref[:, n * d:(n + 1) * d], preferred_element_type=F32))
        up = jnp.dot(br_ref[...], wb_ref[n], preferred_element_type=F32)
        merged = gate * up if merged is None else merged + gate * up
    o_ref[...] = x + jnp.dot(merged.astype(BF16), wo_ref[...], preferred_element_type=F32)


def _merge(x2, norm_g, branches, w_gate, w_branch, w_out, layer, tm):
    t, d = x2.shape
    bw = BRANCH_WIDTH
    br_spec = pl.BlockSpec((tm, bw), lambda i: (i, 0))
    return pl.pallas_call(
        _merge_kernel,
        grid=(t // tm,),
        in_specs=[
            pl.BlockSpec((tm, d), lambda i: (i, 0)),
            _resident((None, 1, d), lambda i: (layer, 0, 0)),
            br_spec, br_spec, br_spec, br_spec,
            _resident((None, d, N_BRANCH * d), lambda i: (layer, 0, 0)),
            _resident((None, N_BRANCH, bw, d), lambda i: (layer, 0, 0, 0)),
            _resident((None, d, d), lambda i: (layer, 0, 0)),
        ],
        out_specs=pl.BlockSpec((tm, d), lambda i: (i, 0)),
        out_shape=jax.ShapeDtypeStruct((t, d), F32),
        compiler_params=_cparams(1),
        name="merge",
    )(x2, norm_g, *branches, w_gate, w_branch, w_out)


def _ffn_kernel(x_ref, g_ref, w1_ref, w2_ref, o_ref):
    x = x_ref[...]
    ff = w2_ref.shape[0]
    hn = _rms_rows(x, g_ref[...]).astype(BF16)
    ag = jnp.dot(hn, w1_ref[...], preferred_element_type=F32)
    a, g = ag[:, :ff], ag[:, ff:]
    u = (a * jax.nn.sigmoid(a) * g).astype(BF16)
    o_ref[...] = x + jnp.dot(u, w2_ref[...], preferred_element_type=F32)


def _ffn(x2, norm_g, w1, w2, layer, tm):
    t, d = x2.shape
    ff = w2.shape[1]
    return pl.pallas_call(
        _ffn_kernel,
        grid=(t // tm,),
        in_specs=[
            pl.BlockSpec((tm, d), lambda i: (i, 0)),
            _resident((None, 1, d), lambda i: (layer, 0, 0)),
            _resident((None, d, 2 * ff), lambda i: (layer, 0, 0)),
            _resident((None, ff, d), lambda i: (layer, 0, 0)),
        ],
        out_specs=pl.BlockSpec((tm, d), lambda i: (i, 0)),
        out_shape=jax.ShapeDtypeStruct((t, d), F32),
        compiler_params=_cparams(1),
        name="ffn",
    )(x2, norm_g, w1, w2)


def _rotary_perm():
    half = HEAD_DIM // 2
    first = [h * HEAD_DIM + i for h in range(N_HEADS) for i in range(half)]
    second = [h * HEAD_DIM + half + i for h in range(N_HEADS) for i in range(half)]
    return np.array(first + second)


def _block_diag(w):
    n_layers, g, c, e = w.shape
    eye = jnp.eye(g, dtype=w.dtype)
    return jnp.einsum('lgce,gk->lgcke', w, eye).reshape(n_layers, g * c, g * e)


def kernel(x, mem, norm_mix_g, norm_mem_g, w_in, w_gate, ret_decay_fwd, ret_decay_bwd, ret_norm_g, pool_w,
           pool_scale, na_q_norm_g, na_k_norm_g, na_rpb, mem_q_norm_g, mem_k_norm_g, w_mem_kv, w_branch, w_out,
           norm_ffn_g, w_ffn_in, w_ffn_out):
    b, s, d = x.shape
    n_layers = w_in.shape[0]
    bw = BRANCH_WIDTH

    perm = _rotary_perm()
    cols = np.arange(w_in.shape[-1])
    cols[0:bw] = perm
    cols[bw:2 * bw] = bw + perm
    w_in_b = w_in[:, :, cols].astype(BF16)
    w_gate_b = w_gate.astype(BF16)
    w_branch_b = w_branch.astype(BF16)
    w_out_b = w_out.astype(BF16)
    w_ffn_in_b = w_ffn_in.astype(BF16)
    w_ffn_out_b = w_ffn_out.astype(BF16)
    w_mem_kv_b = w_mem_kv.astype(BF16)
    pool_w_bd = _block_diag(pool_w).astype(BF16)

    row3 = lambda a: a.astype(F32).reshape(n_layers, 1, -1)
    tile_heads = lambda g: jnp.tile(g.astype(F32), (1, N_HEADS)).reshape(n_layers, 1, bw)
    norm_mix_g3, norm_mem_g3, norm_ffn_g3 = row3(norm_mix_g), row3(norm_mem_g), row3(norm_ffn_g)
    ret_norm_g3, pool_scale3 = row3(ret_norm_g), row3(pool_scale)
    na_gq, na_gk = tile_heads(na_q_norm_g), tile_heads(na_k_norm_g)
    mem_gq, mem_gk = tile_heads(mem_q_norm_g), tile_heads(mem_k_norm_g)
    dec = jnp.concatenate([ret_decay_fwd, ret_decay_bwd], axis=1).astype(F32)
    dec = jnp.broadcast_to(dec[:, :, None], (n_layers, 2 * N_HEADS, 128))
    bias_pat = _na_bias_patterns(na_rpb, s // GRID_W)

    half = HEAD_DIM // 2
    inv = ROPE_THETA ** (-jnp.arange(half, dtype=F32) / half)
    ang = jnp.arange(s, dtype=F32)[:, None] * inv[None, :]
    cos = jnp.tile(jnp.cos(ang), (1, N_HEADS))
    sin = jnp.tile(jnp.sin(ang), (1, N_HEADS))

    mk_all, mv_all = _memkv(mem, norm_mem_g3, w_mem_kv_b, mem_gk)

    x2 = x.reshape(b * s, d)
    for layer in range(n_layers):
        proj = _inproj(x2, norm_mix_g3, w_in_b, layer, tm=512).reshape(b, s, -1)
        ret = _retention(proj, cos, sin, dec, ret_norm_g3, layer)
        pool = _pooling(proj, pool_w_bd, pool_scale3, layer)
        na = _neighbourhood(proj, na_gq, na_gk, bias_pat, layer)
        mo = _memattn(proj, mem_gq, mk_all, mv_all, layer, tq=512)
        branches = [t.reshape(b * s, bw) for t in (ret, pool, na, mo)]
        x2 = _merge(x2, norm_mix_g3, branches, w_gate_b, w_branch_b, w_out_b, layer, tm=512)
        x2 = _ffn(x2, norm_ffn_g3, w_ffn_in_b, w_ffn_out_b, layer, tm=512)
    return x2.reshape(b, s, d)
```

```python
import functools

import numpy as np
import jax
import jax.numpy as jnp
from jax import lax
from jax.experimental import pallas as pl
from jax.experimental.pallas import tpu as pltpu

F32 = jnp.float32
BF16 = jnp.bfloat16

HEAD_DIM = 64
BRANCH_WIDTH = 256
N_HEADS = BRANCH_WIDTH // HEAD_DIM
N_BRANCH = 4
GRID_W = 64
RET_CHUNK = 128
ROPE_THETA = 10000.0
POOL_WINDOWS = (2, 4, 8, 16)
POOL_PAD = 16
NA_WIN_ROWS = 8
NA_WIN_COLS = 16
NEG_INF = -1e30
EPS = 1e-6
QK_SCALE = HEAD_DIM ** -0.5

VMEM_LIMIT_BYTES = 56 * 1024 * 1024


def _cparams(n_grid_dims):
    return pltpu.CompilerParams(
        dimension_semantics=("arbitrary",) * n_grid_dims,
        vmem_limit_bytes=VMEM_LIMIT_BYTES,
    )


def _resident(block_shape, index_map):
    return pl.BlockSpec(block_shape, index_map, pipeline_mode=pl.Buffered(1))


def _rms_rows(x, g):
    ms = jnp.mean(x * x, axis=-1, keepdims=True)
    return x * lax.rsqrt(ms + EPS) * g


def _head_id(shape, axis):
    return lax.shift_right_logical(lax.broadcasted_iota(jnp.int32, shape, axis), 6)


def _head_block_ones():
    shape = (BRANCH_WIDTH, BRANCH_WIDTH)
    return (_head_id(shape, 0) == _head_id(shape, 1)).astype(BF16)


def _head_sumsq(t, ones_bd):
    t2 = t * t
    hi = t2.astype(BF16)
    lo = (t2 - hi.astype(F32)).astype(BF16)
    return (jnp.dot(hi, ones_bd, preferred_element_type=F32)
            + jnp.dot(lo, ones_bd, preferred_element_type=F32))


def _head_rms(t, g, ones_bd):
    ms = _head_sumsq(t, ones_bd) * (1.0 / HEAD_DIM)
    return t * lax.rsqrt(ms + EPS) * g


def _log_sigmoid(x):
    return jnp.minimum(x, 0.0) - jnp.log1p(jnp.exp(-jnp.abs(x)))


def _dot_nt(a, b):
    return lax.dot_general(a, b, (((1,), (1,)), ((), ())), preferred_element_type=F32)


def _dot_tn(a, b):
    return lax.dot_general(a, b, (((0,), (0,)), ((), ())), preferred_element_type=F32)


def _inproj_kernel(x_ref, g_ref, w_ref, o_ref):
    h = _rms_rows(x_ref[...], g_ref[...]).astype(BF16)
    o_ref[...] = jnp.dot(h, w_ref[...], preferred_element_type=F32).astype(o_ref.dtype)


def _inproj(x2, norm_g, w_in, layer, tm):
    t, d = x2.shape
    n_out = w_in.shape[-1]
    return pl.pallas_call(
        _inproj_kernel,
        grid=(t // tm,),
        in_specs=[
            pl.BlockSpec((tm, d), lambda i: (i, 0)),
            _resident((None, 1, d), lambda i: (layer, 0, 0)),
            _resident((None, d, n_out), lambda i: (layer, 0, 0)),
        ],
        out_specs=pl.BlockSpec((tm, n_out), lambda i: (i, 0)),
        out_shape=jax.ShapeDtypeStruct((t, n_out), BF16),
        compiler_params=_cparams(1),
        name="inproj",
    )(x2, norm_g, w_in)


def _ret_kernel(q_ref, k_ref, v_ref, g_ref, cos_ref, sin_ref, dec_ref, gn_ref, o_ref,
                qs_ref, ks_ref, sf_ref, sb_ref, st_ref, dm_ref):
    c = RET_CHUNK
    n_chunks = q_ref.shape[0] // c
    half = BRANCH_WIDTH // 2

    lane = lax.broadcasted_iota(jnp.int32, (1, BRANCH_WIDTH), 1)
    head_qk = lax.shift_right_logical(jnp.bitwise_and(lane, half - 1), 5)
    head_v = lax.shift_right_logical(lane, 6)

    lg = _log_sigmoid(dec_ref[...])
    lg2 = jnp.concatenate([lg, lg], axis=1)

    def per_lane(head_vec, row0):
        out = jnp.zeros((1, BRANCH_WIDTH), F32)
        for h in range(N_HEADS):
            out = jnp.where(head_vec == h, lg2[row0 + h:row0 + h + 1, :], out)
        return out

    lgf_qk, lgb_qk = per_lane(head_qk, 0), per_lane(head_qk, N_HEADS)
    lgf_v, lgb_v = per_lane(head_v, 0), per_lane(head_v, N_HEADS)

    idx = lax.broadcasted_iota(jnp.int32, (c, 1), 0).astype(F32)
    kdec_f = jnp.exp((c - 1 - idx) * lgf_qk)
    qdec_f = jnp.exp((idx + 1) * lgf_qk)
    kdec_b = jnp.exp(idx * lgb_qk)
    qdec_b = jnp.exp((c - idx) * lgb_qk)
    cdec_f = jnp.exp(c * lgf_v)
    cdec_b = jnp.exp(c * lgb_v)

    diff = (lax.broadcasted_iota(jnp.int32, (c, c), 0)
            - lax.broadcasted_iota(jnp.int32, (c, c), 1)).astype(F32)
    for h in range(N_HEADS):
        lf = lg[h:h + 1, :]
        lb = lg[N_HEADS + h:N_HEADS + h + 1, :]
        dm_ref[h] = jnp.where(diff >= 0, jnp.exp(jnp.maximum(diff, 0.0) * lf),
                              jnp.exp(jnp.maximum(-diff, 0.0) * lb))

    row_head = lax.shift_right_logical(
        jnp.bitwise_and(lax.broadcasted_iota(jnp.int32, (BRANCH_WIDTH, BRANCH_WIDTH), 0), half - 1), 5)
    col_head = _head_id((BRANCH_WIDTH, BRANCH_WIDTH), 1)
    same_head = row_head == col_head

    def rotate(t, cos, sin, scale):
        a, b = t[:, :half], t[:, half:]
        return jnp.concatenate([a * cos - b * sin, a * sin + b * cos], axis=1) * scale

    st_ref[...] = jnp.zeros_like(st_ref)

    def fwd_body(j, carry):
        rows = pl.ds(pl.multiple_of(j * c, c), c)
        cos, sin = cos_ref[rows, :], sin_ref[rows, :]
        q = rotate(q_ref[rows, :].astype(F32), cos, sin, QK_SCALE)
        k = rotate(k_ref[rows, :].astype(F32), cos, sin, 1.0)
        qs_ref[rows, :] = q
        ks_ref[rows, :] = k
        kv = _dot_tn((k * kdec_f).astype(BF16), v_ref[rows, :])
        state = st_ref[...]
        sf_ref[j] = state.astype(BF16)
        st_ref[...] = cdec_f * state + jnp.where(same_head, kv, 0.0)
        return carry

    lax.fori_loop(0, n_chunks, fwd_body, 0, unroll=4)

    st_ref[...] = jnp.zeros_like(st_ref)

    def bwd_body(i, carry):
        j = n_chunks - 1 - i
        rows = pl.ds(pl.multiple_of(j * c, c), c)
        kv = _dot_tn((ks_ref[rows, :] * kdec_b).astype(BF16), v_ref[rows, :])
        state = st_ref[...]
        sb_ref[j] = state.astype(BF16)
        st_ref[...] = cdec_b * state + jnp.where(same_head, kv, 0.0)
        return carry

    lax.fori_loop(0, n_chunks, bwd_body, 0, unroll=4)

    ones_bd = _head_block_ones()
    gn = gn_ref[...]

    def out_body(j, carry):
        rows = pl.ds(pl.multiple_of(j * c, c), c)
        q = qs_ref[rows, :]
        kb = ks_ref[rows, :].astype(BF16)
        v = v_ref[rows, :]
        acc = jnp.dot((q * qdec_f).astype(BF16), sf_ref[j], preferred_element_type=F32)
        acc = acc + jnp.dot((q * qdec_b).astype(BF16), sb_ref[j], preferred_element_type=F32)
        for h in range(N_HEADS):
            qm = jnp.where(head_qk == h, q, 0.0).astype(BF16)
            s = _dot_nt(qm, kb) * dm_ref[h]
            pv = jnp.dot(s.astype(BF16), v, preferred_element_type=F32)
            acc = acc + jnp.where(head_v == h, pv, 0.0)
        y = _head_rms(acc, gn, ones_bd)
        gate = g_ref[rows, :].astype(F32)
        o_ref[rows, :] = (y * (gate * jax.nn.sigmoid(gate))).astype(o_ref.dtype)
        return carry

    lax.fori_loop(0, n_chunks, out_body, 0, unroll=4)


def _retention(proj, cos, sin, dec, gn, layer):
    b, s, _ = proj.shape
    bw = BRANCH_WIDTH
    n_chunks = s // RET_CHUNK
    col = lambda cidx: pl.BlockSpec((None, s, bw), lambda i: (i, 0, cidx))
    return pl.pallas_call(
        _ret_kernel,
        grid=(b,),
        in_specs=[
            col(0), col(1), col(2), col(3),
            _resident((s, bw // 2), lambda i: (0, 0)),
            _resident((s, bw // 2), lambda i: (0, 0)),
            _resident((None, 2 * N_HEADS, 128), lambda i: (layer, 0, 0)),
            _resident((None, 1, bw), lambda i: (layer, 0, 0)),
        ],
        out_specs=pl.BlockSpec((None, s, bw), lambda i: (i, 0, 0)),
        out_shape=jax.ShapeDtypeStruct((b, s, bw), BF16),
        scratch_shapes=[
            pltpu.VMEM((s, bw), F32),
            pltpu.VMEM((s, bw), F32),
            pltpu.VMEM((n_chunks, bw, bw), BF16),
            pltpu.VMEM((n_chunks, bw, bw), BF16),
            pltpu.VMEM((bw, bw), F32),
            pltpu.VMEM((N_HEADS, RET_CHUNK, RET_CHUNK), F32),
        ],
        compiler_params=_cparams(1),
        name="retention",
    )(proj, proj, proj, proj, cos, sin, dec, gn)


def _pool_kernel(v_ref, w_ref, sc_ref, o_ref, pad_ref):
    s = v_ref.shape[0]
    pad = POOL_PAD
    tile = 512
    ext = tile + 2 * pad

    pad_ref[0:pad, :] = jnp.zeros((pad, BRANCH_WIDTH), F32)
    pad_ref[pad + s:pad + s + pad, :] = jnp.zeros((pad, BRANCH_WIDTH), F32)
    pad_ref[pad:pad + s, :] = v_ref[...].astype(F32)

    grp = _head_id((1, BRANCH_WIDTH), 1)
    halfw = jnp.where(grp == 0, POOL_WINDOWS[0] // 2,
                      jnp.where(grp == 1, POOL_WINDOWS[1] // 2,
                                jnp.where(grp == 2, POOL_WINDOWS[2] // 2, POOL_WINDOWS[3] // 2)))

    for ci in range(s // tile):
        p = pad_ref[ci * tile:ci * tile + ext, :]
        s2 = p + pltpu.roll(p, 1, 0)
        s4 = pltpu.roll(s2, 1, 0) + pltpu.roll(s2, ext - 1, 0)
        s8 = pltpu.roll(s4, 2, 0) + pltpu.roll(s4, ext - 2, 0)
        s16 = pltpu.roll(s8, 4, 0) + pltpu.roll(s8, ext - 4, 0)
        win = jnp.where(grp == 0, s2, jnp.where(grp == 1, s4, jnp.where(grp == 2, s8, s16)))
        win = win[pad:pad + tile, :]
        t = ci * tile + lax.broadcasted_iota(jnp.int32, (tile, 1), 0)
        count = (jnp.minimum(t + halfw, s) - jnp.maximum(t - halfw, 0)).astype(F32)
        pooled = win / count - p[pad:pad + tile, :]
        mixed = jnp.dot(pooled.astype(BF16), w_ref[...], preferred_element_type=F32) * sc_ref[...]
        o_ref[ci * tile:(ci + 1) * tile, :] = mixed.astype(o_ref.dtype)


def _pooling(proj, w_bd, scale, layer):
    b, s, _ = proj.shape
    bw = BRANCH_WIDTH
    return pl.pallas_call(
        _pool_kernel,
        grid=(b,),
        in_specs=[
            pl.BlockSpec((None, s, bw), lambda i: (i, 0, 4)),
            _resident((None, bw, bw), lambda i: (layer, 0, 0)),
            _resident((None, 1, bw), lambda i: (layer, 0, 0)),
        ],
        out_specs=pl.BlockSpec((None, s, bw), lambda i: (i, 0, 0)),
        out_shape=jax.ShapeDtypeStruct((b, s, bw), BF16),
        scratch_shapes=[pltpu.VMEM((s + 2 * POOL_PAD, bw), F32)],
        compiler_params=_cparams(1),
        name="pooling",
    )(proj, w_bd, scale)


def _stack_heads(t, head_vec):
    return jnp.concatenate([jnp.where(head_vec == h, t, jnp.zeros_like(t)) for h in range(N_HEADS)], axis=0)


def _unstack_heads(t, head_vec, n):
    out = jnp.where(head_vec == 0, t[0:n], 0.0)
    for h in range(1, N_HEADS):
        out = out + jnp.where(head_vec == h, t[h * n:(h + 1) * n], 0.0)
    return out


def _na_kernel(q_ref, k_ref, v_ref, gq_ref, gk_ref, bias_ref, o_ref, qn_ref, kn_ref):
    s = q_ref.shape[0]
    rows = s // GRID_W
    wr = NA_WIN_ROWS
    ones_bd = _head_block_ones()
    head_v = _head_id((1, BRANCH_WIDTH), 1)

    tile = 512
    for ci in range(s // tile):
        sl = slice(ci * tile, (ci + 1) * tile)
        qn_ref[sl, :] = (_head_rms(q_ref[sl, :].astype(F32), gq_ref[...], ones_bd) * QK_SCALE).astype(BF16)
        kn_ref[sl, :] = _head_rms(k_ref[sl, :].astype(F32), gk_ref[...], ones_bd).astype(BF16)

    def row_body(r, carry):
        r0 = jnp.clip(r - wr // 2, 0, rows - wr)
        pat = jnp.where(r < wr // 2, r, jnp.where(r > rows - wr // 2, r - (rows - wr), wr // 2))
        qrows = pl.ds(pl.multiple_of(r * GRID_W, GRID_W), GRID_W)
        krows = pl.ds(pl.multiple_of(r0 * GRID_W, GRID_W), wr * GRID_W)
        qst = _stack_heads(qn_ref[qrows, :], head_v)
        sc = _dot_nt(qst, kn_ref[krows, :]) + bias_ref[pat]
        m = jnp.max(sc, axis=-1, keepdims=True)
        e = jnp.exp(sc - m)
        l = jnp.sum(e, axis=-1, keepdims=True)
        pv = jnp.dot(e.astype(BF16), v_ref[krows, :], preferred_element_type=F32) / l
        o_ref[qrows, :] = _unstack_heads(pv, head_v, GRID_W).astype(o_ref.dtype)
        return carry

    lax.fori_loop(0, rows, row_body, 0, unroll=4)


def _neighbourhood(proj, gq, gk, bias_pat, layer):
    b, s, _ = proj.shape
    bw = BRANCH_WIDTH
    n_pat, mq, nk = bias_pat.shape[1:]
    col = lambda cidx: pl.BlockSpec((None, s, bw), lambda i: (i, 0, cidx))
    return pl.pallas_call(
        _na_kernel,
        grid=(b,),
        in_specs=[
            col(5), col(6), col(7),
            _resident((None, 1, bw), lambda i: (layer, 0, 0)),
            _resident((None, 1, bw), lambda i: (layer, 0, 0)),
            _resident((None, n_pat, mq, nk), lambda i: (layer, 0, 0, 0)),
        ],
        out_specs=pl.BlockSpec((None, s, bw), lambda i: (i, 0, 0)),
        out_shape=jax.ShapeDtypeStruct((b, s, bw), BF16),
        scratch_shapes=[pltpu.VMEM((s, bw), BF16), pltpu.VMEM((s, bw), BF16)],
        compiler_params=_cparams(1),
        name="neighbourhood",
    )(proj, proj, proj, gq, gk, bias_pat)


def _na_bias_patterns(rpb, rows):
    wr, wc = NA_WIN_ROWS, NA_WIN_COLS
    rep_rows = np.array(list(range(wr // 2)) + [wr // 2] + list(range(rows - wr // 2 + 1, rows)))
    row_start = np.clip(rep_rows - wr // 2, 0, rows - wr)
    row_off = row_start[:, None] + np.arange(wr)[None, :] - rep_rows[:, None]
    qc = np.arange(GRID_W)[:, None]
    kc = np.arange(GRID_W)[None, :]
    qwin = np.clip(qc - wc // 2, 0, GRID_W - wc)
    col_mask = (kc >= qwin) & (kc < qwin + wc)
    col_off = np.clip(kc - qc, -(wc - 1), wc - 1) + wc - 1
    n_layers, n_heads, n_dr, n_dc = rpb.shape
    head = np.arange(n_heads)[None, :, None, None, None]
    flat = ((head * n_dr + (row_off + wr - 1)[:, None, None, :, None]) * n_dc
            + col_off[None, None, :, None, :])
    masked_slot = n_heads * n_dr * n_dc
    flat = np.where(col_mask[None, None, :, None, :], flat, masked_slot)
    flat = flat.reshape(len(rep_rows), n_heads * GRID_W, wr * GRID_W).astype(np.int32)
    table = jnp.concatenate([rpb.reshape(n_layers, -1).astype(F32),
                             jnp.full((n_layers, 1), NEG_INF, F32)], axis=1)
    return jnp.take(table, flat, axis=1)


def _memkv_kernel(m_ref, g_ref, w_ref, gk_ref, k_ref, v_ref):
    mn = _rms_rows(m_ref[...], g_ref[...]).astype(BF16)
    kv = jnp.dot(mn, w_ref[...], preferred_element_type=F32)
    k_ref[...] = _head_rms(kv[:, :BRANCH_WIDTH], gk_ref[...], _head_block_ones()).astype(k_ref.dtype)
    v_ref[...] = kv[:, BRANCH_WIDTH:].astype(v_ref.dtype)


def _memkv(mem, norm_g, w_kv, gk):
    b, m, d = mem.shape
    n_layers = w_kv.shape[0]
    bw = BRANCH_WIDTH
    out = jax.ShapeDtypeStruct((n_layers, b, m, bw), BF16)
    return pl.pallas_call(
        _memkv_kernel,
        grid=(n_layers, b),
        in_specs=[
            pl.BlockSpec((None, m, d), lambda l, i: (i, 0, 0)),
            pl.BlockSpec((None, 1, d), lambda l, i: (l, 0, 0)),
            pl.BlockSpec((None, d, 2 * bw), lambda l, i: (l, 0, 0)),
            pl.BlockSpec((None, 1, bw), lambda l, i: (l, 0, 0)),
        ],
        out_specs=[pl.BlockSpec((None, None, m, bw), lambda l, i: (l, i, 0, 0))] * 2,
        out_shape=[out, out],
        compiler_params=_cparams(2),
        name="memkv",
    )(mem, norm_g, w_kv, gk)


def _memattn_kernel(q_ref, gq_ref, k_ref, v_ref, o_ref):
    tq = q_ref.shape[0]
    sub = 128
    head_v = _head_id((1, BRANCH_WIDTH), 1)
    qn = (_head_rms(q_ref[...].astype(F32), gq_ref[...], _head_block_ones()) * QK_SCALE).astype(BF16)
    mk = k_ref[...]
    mv = v_ref[...]
    for ti in range(tq // sub):
        qst = _stack_heads(qn[ti * sub:(ti + 1) * sub, :], head_v)
        sc = _dot_nt(qst, mk)
        m = jnp.max(sc, axis=-1, keepdims=True)
        e = jnp.exp(sc - m)
        l = jnp.sum(e, axis=-1, keepdims=True)
        pv = jnp.dot(e.astype(BF16), mv, preferred_element_type=F32) / l
        o_ref[ti * sub:(ti + 1) * sub, :] = _unstack_heads(pv, head_v, sub).astype(o_ref.dtype)


def _memattn(proj, gq, mk, mv, layer, tq):
    b, s, _ = proj.shape
    bw = BRANCH_WIDTH
    m = mk.shape[2]
    return pl.pallas_call(
        _memattn_kernel,
        grid=(b, s // tq),
        in_specs=[
            pl.BlockSpec((None, tq, bw), lambda i, j: (i, j, 8)),
            _resident((None, 1, bw), lambda i, j: (layer, 0, 0)),
            pl.BlockSpec((None, None, m, bw), lambda i, j: (layer, i, 0, 0)),
            pl.BlockSpec((None, None, m, bw), lambda i, j: (layer, i, 0, 0)),
        ],
        out_specs=pl.BlockSpec((None, tq, bw), lambda i, j: (i, j, 0)),
        out_shape=jax.ShapeDtypeStruct((b, s, bw), BF16),
        compiler_params=_cparams(2),
        name="memattn",
    )(proj, gq, mk, mv)


def _merge_kernel(x_ref, g_ref, b0_ref, b1_ref, b2_ref, b3_ref, wg_ref, wb_ref, wo_ref, o_ref):
    x = x_ref[...]
    d = x.shape[-1]
    h = _rms_rows(x, g_ref[...]).astype(BF16)
    merged = None
    for n, br_ref in enumerate((b0_ref, b1_ref, b2_ref, b3_ref)):
        gate = jax.nn.sigmoid(jnp.dot(h, wg_ref[:, n * d:(n + 1) * d], preferred_element_type=F32))
        up = jnp.dot(br_ref[...], wb_ref[n], preferred_element_type=F32)
        merged = gate * up if merged is None else merged + gate * up
    o_ref[...] = x + jnp.dot(merged.astype(BF16), wo_ref[...], preferred_element_type=F32)


def _merge(x2, norm_g, branches, w_gate, w_branch, w_out, layer, tm):
    t, d = x2.shape
    bw = BRANCH_WIDTH
    br_spec = pl.BlockSpec((tm, bw), lambda i: (i, 0))
    return pl.pallas_call(
        _merge_kernel,
        grid=(t // tm,),
        in_specs=[
            pl.BlockSpec((tm, d), lambda i: (i, 0)),
            _resident((None, 1, d), lambda i: (layer, 0, 0)),
            br_spec, br_spec, br_spec, br_spec,
            _resident((None, d, N_BRANCH * d), lambda i: (layer, 0, 0)),
            _resident((None, N_BRANCH, bw, d), lambda i: (layer, 0, 0, 0)),
            _resident((None, d, d), lambda i: (layer, 0, 0)),
        ],
        out_specs=pl.BlockSpec((tm, d), lambda i: (i, 0)),
        out_shape=jax.ShapeDtypeStruct((t, d), F32),
        compiler_params=_cparams(1),
        name="merge",
    )(x2, norm_g, *branches, w_gate, w_branch, w_out)


def _ffn_kernel(x_ref, g_ref, w1_ref, w2_ref, o_ref):
    x = x_ref[...]
    ff = w2_ref.shape[0]
    hn = _rms_rows(x, g_ref[...]).astype(BF16)
    ag = jnp.dot(hn, w1_ref[...], preferred_element_type=F32)
    a, g = ag[:, :ff], ag[:, ff:]
    u = (a * jax.nn.sigmoid(a) * g).astype(BF16)
    o_ref[...] = x + jnp.dot(u, w2_ref[...], preferred_element_type=F32)


def _ffn(x2, norm_g, w1, w2, layer, tm):
    t, d = x2.shape
    ff = w2.shape[1]
    return pl.pallas_call(
        _ffn_kernel,
        grid=(t // tm,),
        in_specs=[
            pl.BlockSpec((tm, d), lambda i: (i, 0)),
            _resident((None, 1, d), lambda i: (layer, 0, 0)),
            _resident((None, d, 2 * ff), lambda i: (layer, 0, 0)),
            _resident((None, ff, d), lambda i: (layer, 0, 0)),
        ],
        out_specs=pl.BlockSpec((tm, d), lambda i: (i, 0)),
        out_shape=jax.ShapeDtypeStruct((t, d), F32),
        compiler_params=_cparams(1),
        name="ffn",
    )(x2, norm_g, w1, w2)


def _block_diag(w):
    n_layers, g, c, e = w.shape
    eye = jnp.eye(g, dtype=w.dtype)
    return jnp.einsum('lgce,gk->lgcke', w, eye).reshape(n_layers, g * c, g * e)


def kernel(x, mem, norm_mix_g, norm_mem_g, w_in, w_gate, ret_decay_fwd, ret_decay_bwd, ret_norm_g, pool_w,
           pool_scale, na_q_norm_g, na_k_norm_g, na_rpb, mem_q_norm_g, mem_k_norm_g, w_mem_kv, w_branch, w_out,
           norm_ffn_g, w_ffn_in, w_ffn_out):
    b, s, d = x.shape
    n_layers = w_in.shape[0]
    bw = BRANCH_WIDTH

    half = HEAD_DIM // 2
    w_qk = w_in[:, :, :2 * bw].reshape(n_layers, d, 2, N_HEADS, 2, half)
    w_qk = w_qk.transpose(0, 1, 2, 4, 3, 5).reshape(n_layers, d, 2 * bw)
    w_in_b = jnp.concatenate([w_qk, w_in[:, :, 2 * bw:]], axis=-1).astype(BF16)
    w_gate_b = w_gate.astype(BF16)
    w_branch_b = w_branch.astype(BF16)
    w_out_b = w_out.astype(BF16)
    w_ffn_in_b = w_ffn_in.astype(BF16)
    w_ffn_out_b = w_ffn_out.astype(BF16)
    w_mem_kv_b = w_mem_kv.astype(BF16)
    pool_w_bd = _block_diag(pool_w).astype(BF16)

    row3 = lambda a: a.astype(F32).reshape(n_layers, 1, -1)
    tile_heads = lambda g: jnp.tile(g.astype(F32), (1, N_HEADS)).reshape(n_layers, 1, bw)
    norm_mix_g3, norm_mem_g3, norm_ffn_g3 = row3(norm_mix_g), row3(norm_mem_g), row3(norm_ffn_g)
    ret_norm_g3, pool_scale3 = row3(ret_norm_g), row3(pool_scale)
    na_gq, na_gk = tile_heads(na_q_norm_g), tile_heads(na_k_norm_g)
    mem_gq, mem_gk = tile_heads(mem_q_norm_g), tile_heads(mem_k_norm_g)
    dec = jnp.concatenate([ret_decay_fwd, ret_decay_bwd], axis=1).astype(F32)
    dec = jnp.broadcast_to(dec[:, :, None], (n_layers, 2 * N_HEADS, 128))
    bias_pat = _na_bias_patterns(na_rpb, s // GRID_W)

    inv =ROPE_THETA ** (-jnp.arange(half, dtype=F32) / half)
    ang = jnp.arange(s, dtype=F32)[:, None] * inv[None, :]
    cos = jnp.tile(jnp.cos(ang), (1, N_HEADS))
    sin = jnp.tile(jnp.sin(ang), (1, N_HEADS))

    mk_all, mv_all = _memkv(mem, norm_mem_g3, w_mem_kv_b, mem_gk)

    x2 = x.reshape(b * s, d)
    for layer in range(n_layers):
        proj = _inproj(x2, norm_mix_g3, w_in_b, layer, tm=512).reshape(b, s, -1)
        ret = _retention(proj, cos, sin, dec, ret_norm_g3, layer)
        pool = _pooling(proj, pool_w_bd, pool_scale3, layer)
        na = _neighbourhood(proj, na_gq, na_gk, bias_pat, layer)
        mo = _memattn(proj, mem_gq, mk_all, mv_all, layer, tq=512)
        branches = [t.reshape(b * s, bw) for t in (ret, pool, na, mo)]
        x2 = _merge(x2, norm_mix_g3, branches, w_gate_b, w_branch_b, w_out_b, layer, tm=512)
        x2 = _ffn(x2, norm_ffn_g3, w_ffn_in_b, w_ffn_out_b, layer, tm=512)
    return x2.reshape(b, s, d)
```

```python
import functools

import numpy as np
import jax
import jax.numpy as jnp
from jax import lax
from jax.experimental import pallas as pl
from jax.experimental.pallas import tpu as pltpu

F32 = jnp.float32
BF16 = jnp.bfloat16

HEAD_DIM = 64
BRANCH_WIDTH = 256
N_HEADS = BRANCH_WIDTH // HEAD_DIM
N_BRANCH = 4
GRID_W = 64
RET_CHUNK = 128
RET_GROUP = 8
NA_GROUP = 8
ROPE_THETA = 10000.0
POOL_WINDOWS = (2, 4, 8, 16)
POOL_PAD = 16
NA_WIN_ROWS = 8
NA_WIN_COLS = 16
NEG_INF = -1e30
EPS = 1e-6
QK_SCALE = HEAD_DIM ** -0.5

VMEM_LIMIT_BYTES = 56 * 1024 * 1024


def _cparams(n_grid_dims):
    return pltpu.CompilerParams(
        dimension_semantics=("arbitrary",) * n_grid_dims,
        vmem_limit_bytes=VMEM_LIMIT_BYTES,
    )


def _resident(block_shape, index_map):
    return pl.BlockSpec(block_shape, index_map, pipeline_mode=pl.Buffered(1))


def _rms_rows(x, g):
    ms = jnp.mean(x * x, axis=-1, keepdims=True)
    return x * lax.rsqrt(ms + EPS) * g


def _head_id(shape, axis):
    return lax.shift_right_logical(lax.broadcasted_iota(jnp.int32, shape, axis), 6)


def _head_block_ones():
    shape = (BRANCH_WIDTH, BRANCH_WIDTH)
    return (_head_id(shape, 0) == _head_id(shape, 1)).astype(BF16)


def _head_sumsq(t, ones_bd):
    return jnp.dot((t * t).astype(BF16), ones_bd, preferred_element_type=F32)


def _head_rms(t, g, ones_bd):
    ms = _head_sumsq(t, ones_bd) * (1.0 / HEAD_DIM)
    return t * lax.rsqrt(ms + EPS) * g


def _log_sigmoid(x):
    return jnp.minimum(x, 0.0) - jnp.log1p(jnp.exp(-jnp.abs(x)))


def _dot_nt(a, b):
    return lax.dot_general(a, b, (((1,), (1,)), ((), ())), preferred_element_type=F32)


def _dot_tn(a, b):
    return lax.dot_general(a, b, (((0,), (0,)), ((), ())), preferred_element_type=F32)


def _inproj_kernel(x_ref, g_ref, w_ref, cos_ref, sin_ref, gnq_ref, gnk_ref, gmq_ref, o_ref):
    bw = BRANCH_WIDTH
    half = bw // 2
    h = _rms_rows(x_ref[...], g_ref[...]).astype(BF16)
    p = jnp.dot(h, w_ref[...], preferred_element_type=F32)
    cos, sin = cos_ref[...], sin_ref[...]
    ones_bd = _head_block_ones()

    def store(c0, val):
        o_ref[:, c0:c0 + val.shape[1]] = val.astype(o_ref.dtype)

    for c0, scale in ((0, QK_SCALE), (bw, None)):
        a, b = p[:, c0:c0 + half], p[:, c0 + half:c0 + bw]
        lo, hi = a * cos - b * sin, a * sin + b * cos
        store(c0, lo if scale is None else lo * scale)
        store(c0 + half, hi if scale is None else hi * scale)
    store(2 * bw, p[:, 2 * bw:5 * bw])
    store(5 * bw, _head_rms(p[:, 5 * bw:6 * bw], gnq_ref[...], ones_bd) * QK_SCALE)
    store(6 * bw, _head_rms(p[:, 6 * bw:7 * bw], gnk_ref[...], ones_bd))
    store(7 * bw, p[:, 7 * bw:8 * bw])
    store(8 * bw, _head_rms(p[:, 8 * bw:9 * bw], gmq_ref[...], ones_bd) * QK_SCALE)


def _inproj(x2, norm_g, w_in, cos, sin, na_gq, na_gk, mem_gq, layer, tm):
    t, d = x2.shape
    n_out = w_in.shape[-1]
    bw = BRANCH_WIDTH
    seq_tiles = cos.shape[0] // tm
    head_gain = _resident((None, 1, bw), lambda i: (layer, 0, 0))
    return pl.pallas_call(
        _inproj_kernel,
        grid=(t // tm,),
        in_specs=[
            pl.BlockSpec((tm, d), lambda i: (i, 0)),
            _resident((None, 1, d), lambda i: (layer, 0, 0)),
            _resident((None, d, n_out), lambda i: (layer, 0, 0)),
            pl.BlockSpec((tm, bw // 2), lambda i: (i % seq_tiles, 0)),
            pl.BlockSpec((tm, bw // 2), lambda i: (i % seq_tiles, 0)),
            head_gain, head_gain, head_gain,
        ],
        out_specs=pl.BlockSpec((tm, n_out), lambda i: (i, 0)),
        out_shape=jax.ShapeDtypeStruct((t, n_out), BF16),
        compiler_params=_cparams(1),
        name="inproj",
    )(x2, norm_g, w_in, cos, sin, na_gq, na_gk, mem_gq)


def _ret_kernel(q_ref, k_ref, v_ref, g_ref, dec_ref, gn_ref, o_ref, sf_ref, sb_ref, st_ref, dm_ref):
    c = RET_CHUNK
    n_chunks = q_ref.shape[0] // c
    half = BRANCH_WIDTH // 2

    lane = lax.broadcasted_iota(jnp.int32, (1, BRANCH_WIDTH), 1)
    head_qk = lax.shift_right_logical(jnp.bitwise_and(lane, half - 1), 5)
    head_v = lax.shift_right_logical(lane, 6)

    lg = _log_sigmoid(dec_ref[...])
    lg2 = jnp.concatenate([lg, lg], axis=1)

    def per_lane(head_vec, row0):
        out = jnp.zeros((1, BRANCH_WIDTH), F32)
        for h in range(N_HEADS):
            out = jnp.where(head_vec == h, lg2[row0 + h:row0 + h + 1, :], out)
        return out

    lgf_qk, lgb_qk = per_lane(head_qk, 0), per_lane(head_qk, N_HEADS)
    lgf_v, lgb_v = per_lane(head_v, 0), per_lane(head_v, N_HEADS)

    idx = lax.broadcasted_iota(jnp.int32, (c, 1), 0).astype(F32)
    kdec_f = jnp.exp((c - 1 - idx) * lgf_qk)
    qdec_f = jnp.exp((idx + 1) * lgf_qk)
    kdec_b = jnp.exp(idx * lgb_qk)
    qdec_b = jnp.exp((c - idx) * lgb_qk)
    cdec_f = jnp.exp(c * lgf_v)
    cdec_b = jnp.exp(c * lgb_v)

    diff = (lax.broadcasted_iota(jnp.int32, (c, c), 0)
            - lax.broadcasted_iota(jnp.int32, (c, c), 1)).astype(F32)
    for h in range(N_HEADS):
        lf = lg[h:h + 1, :]
        lb = lg[N_HEADS + h:N_HEADS + h + 1, :]
        dm_ref[:, h * c:(h + 1) * c] = jnp.where(diff >= 0, jnp.exp(jnp.maximum(diff, 0.0) * lf),
                                                 jnp.exp(jnp.maximum(-diff, 0.0) * lb))

    row_head = lax.shift_right_logical(
        jnp.bitwise_and(lax.broadcasted_iota(jnp.int32, (BRANCH_WIDTH, BRANCH_WIDTH), 0), half - 1), 5)
    col_head = _head_id((BRANCH_WIDTH, BRANCH_WIDTH), 1)
    same_head = row_head == col_head

    group = RET_GROUP
    chunk_rows = lambda j: pl.ds(pl.multiple_of(j * c, c), c)

    def sweep(kdec, cdec, states_ref, chunk_of):
        st_ref[...] = jnp.zeros_like(st_ref)

        def body(step, carry):
            chunks = [chunk_of(step * group + g) for g in range(group)]
            kvs = [_dot_tn((k_ref[chunk_rows(j), :].astype(F32) * kdec).astype(BF16), v_ref[chunk_rows(j), :])
                   for j in chunks]
            state = st_ref[...]
            for j, kv in zip(chunks, kvs):
                states_ref[j] = state.astype(BF16)
                state = cdec * state + jnp.where(same_head, kv, 0.0)
            st_ref[...] = state
            return carry

        lax.fori_loop(0, n_chunks // group, body, 0)

    sweep(kdec_f, cdec_f, sf_ref, lambda i: i)
    sweep(kdec_b, cdec_b, sb_ref, lambda i: n_chunks - 1 - i)

    ones_bd = _head_block_ones()
    gn = gn_ref[...]

    def out_body(step, carry):
        chunks = [step * group + g for g in range(group)]
        accs = []
        for j in chunks:
            rows = chunk_rows(j)
            qb = q_ref[rows, :]
            q = qb.astype(F32)
            cross = (jnp.dot((q * qdec_f).astype(BF16), sf_ref[j], preferred_element_type=F32)
                     + jnp.dot((q * qdec_b).astype(BF16), sb_ref[j], preferred_element_type=F32))
            scores = _dot_nt(qb, _stack_heads(k_ref[rows, :], head_qk))
            accs.append((cross, scores))
        accs = [cross + jnp.dot((scores * dm_ref[...]).astype(BF16), _stack_heads(v_ref[chunk_rows(j), :], head_v),
                                preferred_element_type=F32)
                for j, (cross, scores) in zip(chunks, accs)]
        sumsq = [_head_sumsq(acc, ones_bd) for acc in accs]
        for j, acc, ss in zip(chunks, accs, sumsq):
            rows = chunk_rows(j)
            y = acc * lax.rsqrt(ss * (1.0 / HEAD_DIM) + EPS) * gn
            gate = g_ref[rows, :].astype(F32)
            o_ref[rows, :] = (y * (gate * jax.nn.sigmoid(gate))).astype(o_ref.dtype)
        return carry

    lax.fori_loop(0, n_chunks // group, out_body, 0)


def _retention(proj, dec, gn, layer):
    b, s, _ = proj.shape
    bw = BRANCH_WIDTH
    n_chunks = s // RET_CHUNK
    col = lambda cidx: pl.BlockSpec((None, s, bw), lambda i: (i, 0, cidx))
    return pl.pallas_call(
        _ret_kernel,
        grid=(b,),
        in_specs=[
            col(0), col(1), col(2), col(3),
            _resident((None, 2 * N_HEADS, 128), lambda i: (layer, 0, 0)),
            _resident((None, 1, bw), lambda i: (layer, 0, 0)),
        ],
        out_specs=pl.BlockSpec((None, s, bw), lambda i: (i, 0, 0)),
        out_shape=jax.ShapeDtypeStruct((b, s, bw), BF16),
        scratch_shapes=[
            pltpu.VMEM((n_chunks, bw, bw), BF16),
            pltpu.VMEM((n_chunks, bw, bw), BF16),
            pltpu.VMEM((bw, bw), F32),
            pltpu.VMEM((RET_CHUNK, N_HEADS * RET_CHUNK), F32),
        ],
        compiler_params=_cparams(1),
        name="retention",
    )(proj, proj, proj, proj, dec, gn)


def _pool_kernel(v_ref, w_ref, sc_ref, o_ref, pad_ref):
    s = v_ref.shape[0]
    pad = POOL_PAD
    tile = 512
    ext = tile + 2 * pad

    pad_ref[0:pad, :] = jnp.zeros((pad, BRANCH_WIDTH), F32)
    pad_ref[pad + s:pad + s + pad, :] = jnp.zeros((pad, BRANCH_WIDTH), F32)
    pad_ref[pad:pad + s, :] = v_ref[...].astype(F32)

    grp = _head_id((1, BRANCH_WIDTH), 1)
    halfw = jnp.where(grp == 0, POOL_WINDOWS[0] // 2,
                      jnp.where(grp == 1, POOL_WINDOWS[1] // 2,
                                jnp.where(grp == 2, POOL_WINDOWS[2] // 2, POOL_WINDOWS[3] // 2)))

    for ci in range(s // tile):
        p = pad_ref[ci * tile:ci * tile + ext, :]
        s2 = p + pltpu.roll(p, 1, 0)
        s4 = pltpu.roll(s2, 1, 0) + pltpu.roll(s2, ext - 1, 0)
        s8 = pltpu.roll(s4, 2, 0) + pltpu.roll(s4, ext - 2, 0)
        s16 = pltpu.roll(s8, 4, 0) + pltpu.roll(s8, ext - 4, 0)
        win = jnp.where(grp == 0, s2, jnp.where(grp == 1, s4, jnp.where(grp == 2, s8, s16)))
        win = win[pad:pad + tile, :]
        t = ci * tile + lax.broadcasted_iota(jnp.int32, (tile, 1), 0)
        count = (jnp.minimum(t + halfw, s) - jnp.maximum(t - halfw, 0)).astype(F32)
        pooled = win / count - p[pad:pad + tile, :]
        mixed = jnp.dot(pooled.astype(BF16), w_ref[...], preferred_element_type=F32) * sc_ref[...]
        o_ref[ci * tile:(ci + 1) * tile, :] = mixed.astype(o_ref.dtype)


def _pooling(proj, w_bd, scale, layer):
    b, s, _ = proj.shape
    bw = BRANCH_WIDTH
    return pl.pallas_call(
        _pool_kernel,
        grid=(b,),
        in_specs=[
            pl.BlockSpec((None, s, bw), lambda i: (i, 0, 4)),
            _resident((None, bw, bw), lambda i: (layer, 0, 0)),
            _resident((None, 1, bw), lambda i: (layer, 0, 0)),
        ],
        out_specs=pl.BlockSpec((None, s, bw), lambda i: (i, 0, 0)),
        out_shape=jax.ShapeDtypeStruct((b, s, bw), BF16),
        scratch_shapes=[pltpu.VMEM((s + 2 * POOL_PAD, bw), F32)],
        compiler_params=_cparams(1),
        name="pooling",
    )(proj, w_bd, scale)


def _stack_heads(t, head_vec):
    return jnp.concatenate([jnp.where(head_vec == h, t, jnp.zeros_like(t)) for h in range(N_HEADS)], axis=0)


def _unstack_heads(t, head_vec, n):
    out = jnp.where(head_vec == 0, t[0:n], 0.0)
    for h in range(1, N_HEADS):
        out = out + jnp.where(head_vec == h, t[h * n:(h + 1) * n], 0.0)
    return out


def _na_pattern_first_offset(pat, rows):
    wr = NA_WIN_ROWS
    query_row = pat if pat <= wr // 2 else rows - wr + pat
    first_key_row = min(max(query_row - wr // 2, 0), rows - wr)
    return first_key_row - query_row + wr - 1


def _na_build_bias(tab_ref, toep_ref, bias_ref, rows):
    wr, wc = NA_WIN_ROWS, NA_WIN_COLS
    n_rel = N_HEADS * (2 * wr - 1)
    x = jnp.concatenate([jnp.broadcast_to(tab_ref[i:i + 1, :], (GRID_W, 128)) for i in range(n_rel)], axis=0)
    qcol = jnp.bitwise_and(lax.broadcasted_iota(jnp.int32, x.shape, 0), GRID_W - 1)
    for bit in range(GRID_W.bit_length() - 1):
        x = jnp.where(jnp.bitwise_and(qcol, 1 << bit) != 0, pltpu.roll(x, 1 << bit, 1), x)
    toep_ref[...] = x

    lane = lax.broadcasted_iota(jnp.int32, (GRID_W, 128), 1)
    kcol = jnp.bitwise_and(lane, GRID_W - 1)
    qwin = jnp.clip(lax.broadcasted_iota(jnp.int32, (GRID_W, 128), 0) - wc // 2, 0, GRID_W - wc)
    in_window = (kcol >= qwin) & (kcol < qwin + wc)
    first_half = lane < GRID_W
    for pat in range(wr):
        dr0 = _na_pattern_first_offset(pat, rows)
        for h in range(N_HEADS):
            for pair in range(wr // 2):
                i0 = h * (2 * wr - 1) + dr0 + 2 * pair
                t = jnp.where(first_half, toep_ref[i0 * GRID_W:(i0 + 1) * GRID_W, :],
                              toep_ref[(i0 + 1) * GRID_W:(i0 + 2) * GRID_W, :])
                bias_ref[pat, h * GRID_W:(h + 1) * GRID_W, pair * 128:(pair + 1) * 128] = (
                    jnp.where(in_window, t, NEG_INF))


def _na_kernel(q_ref, k_ref, v_ref, tab_ref, o_ref, toep_ref, bias_ref):
    s = q_ref.shape[0]
    rows = s // GRID_W
    wr = NA_WIN_ROWS
    head_v = _head_id((1, BRANCH_WIDTH), 1)

    @pl.when(pl.program_id(0) == 0)
    def _():
        _na_build_bias(tab_ref, toep_ref, bias_ref, rows)

    def group_body(step, carry):
        slices, scores = [], []
        for g in range(NA_GROUP):
            r = step * NA_GROUP + g
            r0 = jnp.clip(r - wr // 2, 0, rows - wr)
            pat = jnp.where(r < wr // 2, r, jnp.where(r > rows - wr // 2, r - (rows - wr), wr // 2))
            qrows = pl.ds(pl.multiple_of(r * GRID_W, GRID_W), GRID_W)
            krows = pl.ds(pl.multiple_of(r0 * GRID_W, GRID_W), wr * GRID_W)
            qst = _stack_heads(q_ref[qrows, :], head_v)
            scores.append(_dot_nt(qst, k_ref[krows, :]) + bias_ref[pat])
            slices.append((qrows, krows))
        probs = []
        for sc in scores:
            e = jnp.exp(sc - jnp.max(sc, axis=-1, keepdims=True))
            probs.append((e.astype(BF16), jnp.sum(e, axis=-1, keepdims=True)))
        pvs = [jnp.dot(e, v_ref[krows, :], preferred_element_type=F32) / l
               for (e, l), (_, krows) in zip(probs, slices)]
        for pv, (qrows, _) in zip(pvs, slices):
            o_ref[qrows, :] = _unstack_heads(pv, head_v, GRID_W).astype(o_ref.dtype)
        return carry

    lax.fori_loop(0, rows // NA_GROUP, group_body, 0)


def _neighbourhood(proj, rpb_tab, layer):
    b, s, _ = proj.shape
    bw = BRANCH_WIDTH
    n_tab = rpb_tab.shape[1]
    col = lambda cidx: pl.BlockSpec((None, s, bw), lambda i: (i, 0, cidx))
    return pl.pallas_call(
        _na_kernel,
        grid=(b,),
        in_specs=[
            col(5), col(6), col(7),
            _resident((None, n_tab, 128), lambda i: (layer, 0, 0)),
        ],
        out_specs=pl.BlockSpec((None, s, bw), lambda i: (i, 0, 0)),
        out_shape=jax.ShapeDtypeStruct((b, s, bw), BF16),
        scratch_shapes=[
            pltpu.VMEM((N_HEADS * (2 * NA_WIN_ROWS - 1) * GRID_W, 128), F32),
            pltpu.VMEM((NA_WIN_ROWS, N_HEADS * GRID_W, NA_WIN_ROWS * GRID_W), F32),
        ],
        compiler_params=_cparams(1),
        name="neighbourhood",
    )(proj, proj, proj, rpb_tab)


def _na_rpb_table(rpb):
    n_layers, n_heads, n_dr, n_dc = rpb.shape
    t = jnp.pad(rpb.astype(F32), ((0, 0), (0, 0), (0, 0), (0, GRID_W - n_dc)))
    t = jnp.roll(t, -(NA_WIN_COLS - 1), axis=-1)
    t = jnp.tile(t, (1, 1, 1, 128 // GRID_W)).reshape(n_layers, n_heads * n_dr, 128)
    return jnp.pad(t, ((0, 0), (0, -(n_heads * n_dr) % 8), (0, 0)))


def _memkv_kernel(m_ref, g_ref, w_ref, gk_ref, k_ref, v_ref):
    mn = _rms_rows(m_ref[...], g_ref[...]).astype(BF16)
    kv = jnp.dot(mn, w_ref[...], preferred_element_type=F32)
    head_v = _head_id((1, BRANCH_WIDTH), 1)
    kn = _head_rms(kv[:, :BRANCH_WIDTH], gk_ref[...], _head_block_ones())
    k_ref[...] = _stack_heads(kn.astype(k_ref.dtype), head_v)
    v_ref[...] = _stack_heads(kv[:, BRANCH_WIDTH:].astype(v_ref.dtype), head_v)


def _memkv(mem, norm_g, w_kv, gk):
    b, m, d = mem.shape
    n_layers = w_kv.shape[0]
    bw = BRANCH_WIDTH
    out = jax.ShapeDtypeStruct((n_layers, b, N_HEADS * m, bw), BF16)
    return pl.pallas_call(
        _memkv_kernel,
        grid=(n_layers, b),
        in_specs=[
            pl.BlockSpec((None, m, d), lambda l, i: (i, 0, 0)),
            pl.BlockSpec((None, 1, d), lambda l, i: (l, 0, 0)),
            pl.BlockSpec((None, d, 2 * bw), lambda l, i: (l, 0, 0)),
            pl.BlockSpec((None, 1, bw), lambda l, i: (l, 0, 0)),
        ],
        out_specs=[pl.BlockSpec((None, None, N_HEADS * m, bw), lambda l, i: (l, i, 0, 0))] * 2,
        out_shape=[out, out],
        compiler_params=_cparams(2),
        name="memkv",
    )(mem, norm_g, w_kv, gk)


def _memattn_kernel(q_ref, k_ref, v_ref, o_ref):
    tq = q_ref.shape[0]
    n_mem = k_ref.shape[0] // N_HEADS
    sub = 128
    head_v = _head_id((1, BRANCH_WIDTH), 1)
    mk = k_ref[...]
    mv = v_ref[...]
    tiles = [slice(ti * sub, (ti + 1) * sub) for ti in range(tq // sub)]
    scores = [_dot_nt(q_ref[t, :], mk) for t in tiles]
    probs, inv_ls = [], []
    for sc in scores:
        es = []
        inv_l = jnp.zeros((sub, BRANCH_WIDTH), F32)
        for h in range(N_HEADS):
            seg = sc[:, h * n_mem:(h + 1) * n_mem]
            e = jnp.exp(seg - jnp.max(seg, axis=-1, keepdims=True))
            inv_l = jnp.where(head_v == h, 1.0 / jnp.sum(e, axis=-1, keepdims=True), inv_l)
            es.append(e.astype(BF16))
        probs.append(jnp.concatenate(es, axis=1))
        inv_ls.append(inv_l)
    pvs = [jnp.dot(p, mv, preferred_element_type=F32) for p in probs]
    for t, pv, inv_l in zip(tiles, pvs, inv_ls):
        o_ref[t, :] = (pv * inv_l).astype(o_ref.dtype)


def _memattn(proj, mk, mv, layer, tq):
    b, s, _ = proj.shape
    bw = BRANCH_WIDTH
    m = mk.shape[2]
    return pl.pallas_call(
        _memattn_kernel,
        grid=(b, s // tq),
        in_specs=[
            pl.BlockSpec((None, tq, bw), lambda i, j: (i, j, 8)),
            pl.BlockSpec((None, None, m, bw), lambda i, j: (layer, i, 0, 0)),
            pl.BlockSpec((None, None, m, bw), lambda i, j: (layer, i, 0, 0)),
        ],
        out_specs=pl.BlockSpec((None, tq, bw), lambda i, j: (i, j, 0)),
        out_shape=jax.ShapeDtypeStruct((b, s, bw), BF16),
        compiler_params=_cparams(2),
        name="memattn",
    )(proj, mk, mv)


def _merge_kernel(x_ref, g_ref, b0_ref, b1_ref, b2_ref, b3_ref, wg_ref, wb_ref, wo_ref, o_ref):
    x = x_ref[...]
    d = x.shape[-1]
    h = _rms_rows(x, g_ref[...]).astype(BF16)
    merged = None
    for n, br_ref in enumerate((b0_ref, b1_ref, b2_ref, b3_ref)):
        gate = jax.nn.sigmoid(jnp.dot(h, wg_ref[:, n * d:(n + 1) * d], preferred_element_type=F32))
        up = jnp.dot(br_ref[...], wb_ref[n], preferred_element_type=F32)
        merged = gate * up if merged is None else merged + gate * up
    o_ref[...] = x + jnp.dot(merged.astype(BF16), wo_ref[...], preferred_element_type=F32)


def _merge(x2, norm_g, branches, w_gate, w_branch, w_out, layer, tm):
    t, d = x2.shape
    bw = BRANCH_WIDTH
    br_spec = pl.BlockSpec((tm, bw), lambda i: (i, 0))
    return pl.pallas_call(
        _merge_kernel,
        grid=(t // tm,),
        in_specs=[
            pl.BlockSpec((tm, d), lambda i: (i, 0)),
            _resident((None, 1, d), lambda i: (layer, 0, 0)),
            br_spec, br_spec, br_spec, br_spec,
            _resident((None, d, N_BRANCH * d), lambda i: (layer, 0, 0)),
            _resident((None, N_BRANCH, bw, d), lambda i: (layer, 0, 0, 0)),
            _resident((None, d, d), lambda i: (layer, 0, 0)),
        ],
        out_specs=pl.BlockSpec((tm, d), lambda i: (i, 0)),
        out_shape=jax.ShapeDtypeStruct((t, d), F32),
        compiler_params=_cparams(1),
        name="merge",
    )(x2, norm_g, *branches, w_gate, w_branch, w_out)


def _ffn_kernel(x_ref, g_ref, w1_ref, w2_ref, o_ref):
    x = x_ref[...]
    ff = w2_ref.shape[0]
    hn = _rms_rows(x, g_ref[...]).astype(BF16)
    ag = jnp.dot(hn, w1_ref[...], preferred_element_type=F32)
    a, g = ag[:, :ff], ag[:, ff:]
    u = (a * jax.nn.sigmoid(a) * g).astype(BF16)
    o_ref[...] = x + jnp.dot(u, w2_ref[...], preferred_element_type=F32)


def _ffn(x2, norm_g, w1, w2, layer, tm):
    t, d = x2.shape
    ff = w2.shape[1]
    return pl.pallas_call(
        _ffn_kernel,
        grid=(t // tm,),
        in_specs=[
            pl.BlockSpec((tm, d), lambda i: (i, 0)),
            _resident((None, 1, d), lambda i: (layer, 0, 0)),
            _resident((None, d, 2 * ff), lambda i: (layer, 0, 0)),
            _resident((None, ff, d), lambda i: (layer, 0, 0)),
        ],
        out_specs=pl.BlockSpec((tm, d), lambda i: (i, 0)),
        out_shape=jax.ShapeDtypeStruct((t, d), F32),
        compiler_params=_cparams(1),
        name="ffn",
    )(x2, norm_g, w1, w2)


def _block_diag(w):
    n_layers, g, c, e = w.shape
    eye = jnp.eye(g, dtype=w.dtype)
    return jnp.einsum('lgce,gk->lgcke', w, eye).reshape(n_layers, g * c, g * e)


def kernel(x, mem, norm_mix_g, norm_mem_g, w_in, w_gate, ret_decay_fwd, ret_decay_bwd, ret_norm_g, pool_w,
           pool_scale, na_q_norm_g, na_k_norm_g, na_rpb, mem_q_norm_g, mem_k_norm_g, w_mem_kv, w_branch, w_out,
           norm_ffn_g, w_ffn_in, w_ffn_out):
    b, s, d = x.shape
    n_layers = w_in.shape[0]
    bw = BRANCH_WIDTH

    half = HEAD_DIM // 2
    w_qk = w_in[:, :, :2 * bw].reshape(n_layers, d, 2, N_HEADS, 2, half)
    w_qk = w_qk.transpose(0, 1, 2, 4, 3, 5).reshape(n_layers, d, 2 * bw)
    w_in_b = jnp.concatenate([w_qk, w_in[:, :, 2 * bw:]], axis=-1).astype(BF16)
    w_gate_b = w_gate.astype(BF16)
    w_branch_b = w_branch.astype(BF16)
    w_out_b = w_out.astype(BF16)
    w_ffn_in_b = w_ffn_in.astype(BF16)
    w_ffn_out_b = w_ffn_out.astype(BF16)
    w_mem_kv_b = w_mem_kv.astype(BF16)
    pool_w_bd = _block_diag(pool_w).astype(BF16)

    row3 = lambda a: a.astype(F32).reshape(n_layers, 1, -1)
    tile_heads = lambda g: jnp.tile(g.astype(F32), (1, N_HEADS)).reshape(n_layers, 1, bw)
    norm_mix_g3, norm_mem_g3, norm_ffn_g3 = row3(norm_mix_g), row3(norm_mem_g), row3(norm_ffn_g)
    ret_norm_g3, pool_scale3 = row3(ret_norm_g), row3(pool_scale)
    na_gq, na_gk = tile_heads(na_q_norm_g), tile_heads(na_k_norm_g)
    mem_gq, mem_gk = tile_heads(mem_q_norm_g), tile_heads(mem_k_norm_g)
    dec = jnp.concatenate([ret_decay_fwd, ret_decay_bwd], axis=1).astype(F32)
    dec = jnp.broadcast_to(dec[:, :, None], (n_layers, 2 * N_HEADS, 128))
    rpb_tab = _na_rpb_table(na_rpb)

    inv = ROPE_THETA ** (-jnp.arange(half, dtype=F32) / half)
    ang = jnp.arange(s, dtype=F32)[:, None] * inv[None, :]
    cos = jnp.tile(jnp.cos(ang), (1, N_HEADS))
    sin = jnp.tile(jnp.sin(ang), (1, N_HEADS))

    mk_all, mv_all = _memkv(mem, norm_mem_g3, w_mem_kv_b, mem_gk)

    x2 = x.reshape(b * s, d)
    for layer in range(n_layers):
        proj = _inproj(x2, norm_mix_g3, w_in_b, cos, sin, na_gq, na_gk, mem_gq, layer, tm=512).reshape(b, s, -1)
        ret = _retention(proj, dec, ret_norm_g3, layer)
        pool = _pooling(proj, pool_w_bd, pool_scale3, layer)
        na = _neighbourhood(proj, rpb_tab, layer)
        mo = _memattn(proj, mk_all, mv_all, layer, tq=512)
        branches = [t.reshape(b * s, bw) for t in (ret, pool, na, mo)]
        x2 = _merge(x2, norm_mix_g3, branches, w_gate_b, w_branch_b, w_out_b, layer, tm=512)
        x2 = _ffn(x2, norm_ffn_g3, w_ffn_in_b, w_ffn_out_b, layer, tm=512)
    return x2.reshape(b, s, d)
```

```python
import functools

import numpy as np
import jax
import jax.numpy as jnp
from jax import lax
from jax.experimental import pallas as pl
from jax.experimental.pallas import tpu as pltpu

F32 = jnp.float32
BF16 = jnp.bfloat16

HEAD_DIM = 64
BRANCH_WIDTH = 256
N_HEADS = BRANCH_WIDTH // HEAD_DIM
N_BRANCH = 4
GRID_W = 64
RET_CHUNK = 128
RET_GROUP = 8
NA_GROUP = 8
MEM_SUB_ROWS = 128
MEM_GROUP = 8
ROPE_THETA = 10000.0
POOL_WINDOWS = (2, 4, 8, 16)
POOL_PAD = 16
NA_WIN_ROWS = 8
NA_WIN_COLS = 16
NEG_INF = -1e30
EPS = 1e-6
QK_SCALE = HEAD_DIM ** -0.5

VMEM_LIMIT_BYTES = 56 * 1024 * 1024

INPROJ_TILE_ROWS = 1024
MERGE_TILE_ROWS = 512
FFN_TILE_ROWS = 512


def _cparams(n_grid_dims):
    return pltpu.CompilerParams(
        dimension_semantics=("arbitrary",) * n_grid_dims,
        vmem_limit_bytes=VMEM_LIMIT_BYTES,
    )


def _resident(block_shape, index_map):
    return pl.BlockSpec(block_shape, index_map, pipeline_mode=pl.Buffered(1))


def _rms_rows(x, g):
    ms = jnp.mean(x * x, axis=-1, keepdims=True)
    return x * lax.rsqrt(ms + EPS) * g


def _head_id(shape, axis):
    return lax.shift_right_logical(lax.broadcasted_iota(jnp.int32, shape, axis), 6)


def _head_block_ones():
    shape = (BRANCH_WIDTH, BRANCH_WIDTH)
    return (_head_id(shape, 0) == _head_id(shape, 1)).astype(BF16)


def _head_sumsq(t, ones_bd):
    return jnp.dot((t * t).astype(BF16), ones_bd, preferred_element_type=F32)


def _head_rms(t, g, ones_bd):
    ms = _head_sumsq(t, ones_bd) * (1.0 / HEAD_DIM)
    return t * lax.rsqrt(ms + EPS) * g


def _row_halves(n):
    return [slice(0, n // 2), slice(n // 2, n)]


def _log_sigmoid(x):
    return jnp.minimum(x, 0.0) - jnp.log1p(jnp.exp(-jnp.abs(x)))


def _dot_nt(a, b):
    return lax.dot_general(a, b, (((1,), (1,)), ((), ())), preferred_element_type=F32)


def _dot_tn(a, b):
    return lax.dot_general(a, b, (((0,), (0,)), ((), ())), preferred_element_type=F32)


def _inproj_kernel(x_ref, g_ref, w_ref, cos_ref, sin_ref, gnq_ref, gnk_ref, gmq_ref, o_ref):
    bw = BRANCH_WIDTH
    half = bw // 2
    ones_bd = _head_block_ones()
    halves = _row_halves(x_ref.shape[0])
    hs = [_rms_rows(x_ref[r, :], g_ref[...]).astype(BF16) for r in halves]
    ps = [jnp.dot(h, w_ref[...], preferred_element_type=F32) for h in hs]
    for r, p in zip(halves, ps):
        cos, sin = cos_ref[r, :], sin_ref[r, :]

        def store(c0, val):
            o_ref[r, c0:c0 + val.shape[1]] = val.astype(o_ref.dtype)

        for c0, scale in ((0, QK_SCALE), (bw, None)):
            a, b = p[:, c0:c0 + half], p[:, c0 + half:c0 + bw]
            lo, hi = a * cos - b * sin, a * sin + b * cos
            store(c0, lo if scale is None else lo * scale)
            store(c0 + half, hi if scale is None else hi * scale)
        store(2 * bw, p[:, 2 * bw:5 * bw])
        store(5 * bw, _head_rms(p[:, 5 * bw:6 * bw], gnq_ref[...], ones_bd) * QK_SCALE)
        store(6 * bw, _head_rms(p[:, 6 * bw:7 * bw], gnk_ref[...], ones_bd))
        store(7 * bw, p[:, 7 * bw:8 * bw])
        store(8 * bw, _head_rms(p[:, 8 * bw:9 * bw], gmq_ref[...], ones_bd) * QK_SCALE)


def _inproj(x2, norm_g, w_in, cos, sin, na_gq, na_gk, mem_gq, layer, tm):
    t, d = x2.shape
    n_out = w_in.shape[-1]
    bw = BRANCH_WIDTH
    seq_tiles = cos.shape[0] // tm
    head_gain = _resident((None, 1, bw), lambda i: (layer, 0, 0))
    return pl.pallas_call(
        _inproj_kernel,
        grid=(t // tm,),
        in_specs=[
            pl.BlockSpec((tm, d), lambda i: (i, 0)),
            _resident((None, 1, d), lambda i: (layer, 0, 0)),
            _resident((None, d, n_out), lambda i: (layer, 0, 0)),
            pl.BlockSpec((tm, bw // 2), lambda i: (i % seq_tiles, 0)),
            pl.BlockSpec((tm, bw // 2), lambda i: (i % seq_tiles, 0)),
            head_gain, head_gain, head_gain,
        ],
        out_specs=pl.BlockSpec((tm, n_out), lambda i: (i, 0)),
        out_shape=jax.ShapeDtypeStruct((t, n_out), BF16),
        compiler_params=_cparams(1),
        name="inproj",
    )(x2, norm_g, w_in, cos, sin, na_gq, na_gk, mem_gq)


def _ret_kernel(q_ref, k_ref, v_ref, g_ref, dec_ref, gn_ref, o_ref, sf_ref, sb_ref, st_ref, dm_ref):
    c = RET_CHUNK
    n_chunks = q_ref.shape[0] // c
    half = BRANCH_WIDTH // 2

    lane = lax.broadcasted_iota(jnp.int32, (1, BRANCH_WIDTH), 1)
    head_qk = lax.shift_right_logical(jnp.bitwise_and(lane, half - 1), 5)
    head_v = lax.shift_right_logical(lane, 6)

    lg = _log_sigmoid(dec_ref[...])
    lg2 = jnp.concatenate([lg, lg], axis=1)

    def per_lane(head_vec, row0):
        out = jnp.zeros((1, BRANCH_WIDTH), F32)
        for h in range(N_HEADS):
            out = jnp.where(head_vec == h, lg2[row0 + h:row0 + h + 1, :], out)
        return out

    lgf_qk, lgb_qk = per_lane(head_qk, 0), per_lane(head_qk, N_HEADS)
    lgf_v, lgb_v = per_lane(head_v, 0), per_lane(head_v, N_HEADS)

    idx = lax.broadcasted_iota(jnp.int32, (c, 1), 0).astype(F32)
    kdec_f = jnp.exp((c - 1 - idx) * lgf_qk)
    qdec_f = jnp.exp((idx + 1) * lgf_qk)
    kdec_b = jnp.exp(idx * lgb_qk)
    qdec_b = jnp.exp((c - idx) * lgb_qk)
    cdec_f = jnp.exp(c * lgf_v)
    cdec_b = jnp.exp(c * lgb_v)

    diff = (lax.broadcasted_iota(jnp.int32, (c, c), 0)
            - lax.broadcasted_iota(jnp.int32, (c, c), 1)).astype(F32)
    for h in range(N_HEADS):
        lf = lg[h:h + 1, :]
        lb = lg[N_HEADS + h:N_HEADS + h + 1, :]
        dm_ref[:, h * c:(h + 1) * c] = jnp.where(diff >= 0, jnp.exp(jnp.maximum(diff, 0.0) * lf),
                                                 jnp.exp(jnp.maximum(-diff, 0.0) * lb))

    row_head = lax.shift_right_logical(
        jnp.bitwise_and(lax.broadcasted_iota(jnp.int32, (BRANCH_WIDTH, BRANCH_WIDTH), 0), half - 1), 5)
    col_head = _head_id((BRANCH_WIDTH, BRANCH_WIDTH), 1)
    same_head = row_head == col_head

    group = RET_GROUP
    chunk_rows = lambda j: pl.ds(pl.multiple_of(j * c, c), c)

    def sweep(kdec, cdec, states_ref, chunk_of):
        st_ref[...] = jnp.zeros_like(st_ref)

        def body(step, carry):
            chunks = [chunk_of(step * group + g) for g in range(group)]
            kvs = [_dot_tn((k_ref[chunk_rows(j), :].astype(F32) * kdec).astype(BF16), v_ref[chunk_rows(j), :])
                   for j in chunks]
            state = st_ref[...]
            for j, kv in zip(chunks, kvs):
                states_ref[j] = state.astype(BF16)
                state = cdec * state + jnp.where(same_head, kv, 0.0)
            st_ref[...] = state
            return carry

        lax.fori_loop(0, n_chunks // group, body, 0)

    sweep(kdec_f, cdec_f, sf_ref, lambda i: i)
    sweep(kdec_b, cdec_b, sb_ref, lambda i: n_chunks - 1 - i)

    ones_bd = _head_block_ones()
    gn = gn_ref[...]

    def out_body(step, carry):
        chunks = [step * group + g for g in range(group)]
        accs = []
        for j in chunks:
            rows = chunk_rows(j)
            qb = q_ref[rows, :]
            q = qb.astype(F32)
            cross = (jnp.dot((q * qdec_f).astype(BF16), sf_ref[j], preferred_element_type=F32)
                     + jnp.dot((q * qdec_b).astype(BF16), sb_ref[j], preferred_element_type=F32))
            scores = _dot_nt(qb, _stack_heads(k_ref[rows, :], head_qk))
            accs.append((cross, scores))
        accs = [cross + jnp.dot((scores * dm_ref[...]).astype(BF16), _stack_heads(v_ref[chunk_rows(j), :], head_v),
                                preferred_element_type=F32)
                for j, (cross, scores) in zip(chunks, accs)]
        sumsq = [_head_sumsq(acc, ones_bd) for acc in accs]
        for j, acc, ss in zip(chunks, accs, sumsq):
            rows = chunk_rows(j)
            y = acc * lax.rsqrt(ss * (1.0 / HEAD_DIM) + EPS) * gn
            gate = g_ref[rows, :].astype(F32)
            o_ref[rows, :] = (y * (gate * jax.nn.sigmoid(gate))).astype(o_ref.dtype)
        return carry

    lax.fori_loop(0, n_chunks // group, out_body, 0)


def _retention(proj, dec, gn, layer):
    b, s, _ = proj.shape
    bw = BRANCH_WIDTH
    n_chunks = s // RET_CHUNK
    col = lambda cidx: pl.BlockSpec((None, s, bw), lambda i: (i, 0, cidx))
    return pl.pallas_call(
        _ret_kernel,
        grid=(b,),
        in_specs=[
            col(0), col(1), col(2), col(3),
            _resident((None, 2 * N_HEADS, 128), lambda i: (layer, 0, 0)),
            _resident((None, 1, bw), lambda i: (layer, 0, 0)),
        ],
        out_specs=pl.BlockSpec((None, s, bw), lambda i: (i, 0, 0)),
        out_shape=jax.ShapeDtypeStruct((b, s, bw), BF16),
        scratch_shapes=[
            pltpu.VMEM((n_chunks, bw, bw), BF16),
            pltpu.VMEM((n_chunks, bw, bw), BF16),
            pltpu.VMEM((bw, bw), F32),
            pltpu.VMEM((RET_CHUNK, N_HEADS * RET_CHUNK), F32),
        ],
        compiler_params=_cparams(1),
        name="retention",
    )(proj, proj, proj, proj, dec, gn)


def _pool_kernel(v_ref, w_ref, sc_ref, o_ref, pad_ref):
    s = v_ref.shape[0]
    pad = POOL_PAD
    tile = 512
    ext = tile + 2 * pad

    pad_ref[0:pad, :] = jnp.zeros((pad, BRANCH_WIDTH), F32)
    pad_ref[pad + s:pad + s + pad, :] = jnp.zeros((pad, BRANCH_WIDTH), F32)
    pad_ref[pad:pad + s, :] = v_ref[...].astype(F32)

    grp = _head_id((1, BRANCH_WIDTH), 1)
    halfw = jnp.where(grp == 0, POOL_WINDOWS[0] // 2,
                      jnp.where(grp == 1, POOL_WINDOWS[1] // 2,
                                jnp.where(grp == 2, POOL_WINDOWS[2] // 2, POOL_WINDOWS[3] // 2)))

    half_lanes = BRANCH_WIDTH // 2
    grp_lo, grp_hi = grp[:, :half_lanes], grp[:, half_lanes:]
    for ci in range(s // tile):
        p = pad_ref[ci * tile:ci * tile + ext, :]
        s2 = p + pltpu.roll(p, 1, 0)
        s4 = pltpu.roll(s2, 1, 0) + pltpu.roll(s2, ext - 1, 0)
        s4_hi = s4[:, half_lanes:]
        s8 = pltpu.roll(s4_hi, 2, 0) + pltpu.roll(s4_hi, ext - 2, 0)
        s16 = pltpu.roll(s8, 4, 0) + pltpu.roll(s8, ext - 4, 0)
        win = jnp.concatenate([jnp.where(grp_lo == 0, s2[:, :half_lanes], s4[:, :half_lanes]),
                               jnp.where(grp_hi == 2, s8, s16)], axis=1)
        win = win[pad:pad + tile, :]
        if 0 < ci < s // tile - 1:
            pooled = win * (0.5 / halfw.astype(F32)) - p[pad:pad + tile, :]
        else:
            t = ci * tile + lax.broadcasted_iota(jnp.int32, (tile, 1), 0)
            count = (jnp.minimum(t + halfw, s) - jnp.maximum(t - halfw, 0)).astype(F32)
            pooled = win / count - p[pad:pad + tile, :]
        mixed = jnp.dot(pooled.astype(BF16), w_ref[...], preferred_element_type=F32) * sc_ref[...]
        o_ref[ci * tile:(ci + 1) * tile, :] = mixed.astype(o_ref.dtype)


def _pooling(proj, w_bd, scale, layer):
    b, s, _ = proj.shape
    bw = BRANCH_WIDTH
    return pl.pallas_call(
        _pool_kernel,
        grid=(b,),
        in_specs=[
            pl.BlockSpec((None, s, bw), lambda i: (i, 0, 4)),
            _resident((None, bw, bw), lambda i: (layer, 0, 0)),
            _resident((None, 1, bw), lambda i: (layer, 0, 0)),
        ],
        out_specs=pl.BlockSpec((None, s, bw), lambda i: (i, 0, 0)),
        out_shape=jax.ShapeDtypeStruct((b, s, bw), BF16),
        scratch_shapes=[pltpu.VMEM((s + 2 * POOL_PAD, bw), F32)],
        compiler_params=_cparams(1),
        name="pooling",
    )(proj, w_bd, scale)


def _stack_heads(t, head_vec):
    return jnp.concatenate([jnp.where(head_vec == h, t, jnp.zeros_like(t)) for h in range(N_HEADS)], axis=0)


def _unstack_heads(t, head_vec, n):
    out = jnp.where(head_vec == 0, t[0:n], 0.0)
    for h in range(1, N_HEADS):
        out = out + jnp.where(head_vec == h, t[h * n:(h + 1) * n], 0.0)
    return out


def _na_pattern_first_offset(pat, rows):
    wr = NA_WIN_ROWS
    query_row = pat if pat <= wr // 2 else rows - wr + pat
    first_key_row = min(max(query_row - wr // 2, 0), rows - wr)
    return first_key_row - query_row + wr - 1


def _na_build_bias(tab_ref, toep_ref, bias_ref, rows):
    wr, wc = NA_WIN_ROWS, NA_WIN_COLS
    n_rel = N_HEADS * (2 * wr - 1)
    x = jnp.concatenate([jnp.broadcast_to(tab_ref[i:i + 1, :], (GRID_W, 128)) for i in range(n_rel)], axis=0)
    qcol = jnp.bitwise_and(lax.broadcasted_iota(jnp.int32, x.shape, 0), GRID_W - 1)
    for bit in range(GRID_W.bit_length() - 1):
        x = jnp.where(jnp.bitwise_and(qcol, 1 << bit) != 0, pltpu.roll(x, 1 << bit, 1), x)
    toep_ref[...] = x

    lane = lax.broadcasted_iota(jnp.int32, (GRID_W, 128), 1)
    kcol = jnp.bitwise_and(lane, GRID_W - 1)
    qwin = jnp.clip(lax.broadcasted_iota(jnp.int32, (GRID_W, 128), 0) - wc // 2, 0, GRID_W - wc)
    in_window = (kcol >= qwin) & (kcol < qwin + wc)
    first_half = lane < GRID_W
    for pat in range(wr):
        dr0 = _na_pattern_first_offset(pat, rows)
        for h in range(N_HEADS):
            for pair in range(wr // 2):
                i0 = h * (2 * wr - 1) + dr0 + 2 * pair
                t = jnp.where(first_half, toep_ref[i0 * GRID_W:(i0 + 1) * GRID_W, :],
                              toep_ref[(i0 + 1) * GRID_W:(i0 + 2) * GRID_W, :])
                bias_ref[pat, h * GRID_W:(h + 1) * GRID_W, pair * 128:(pair + 1) * 128] = (
                    jnp.where(in_window, t, NEG_INF))


def _na_kernel(q_ref, k_ref, v_ref, tab_ref, o_ref, toep_ref, bias_ref):
    s = q_ref.shape[0]
    rows = s // GRID_W
    wr = NA_WIN_ROWS
    head_v = _head_id((1, BRANCH_WIDTH), 1)

    @pl.when(pl.program_id(0) == 0)
    def _():
        _na_build_bias(tab_ref, toep_ref, bias_ref, rows)

    def group_body(step, carry):
        slices, scores = [], []
        for g in range(NA_GROUP):
            r = step * NA_GROUP + g
            r0 = jnp.clip(r - wr // 2, 0, rows - wr)
            pat = jnp.where(r < wr // 2, r, jnp.where(r > rows - wr // 2, r - (rows - wr), wr // 2))
            qrows = pl.ds(pl.multiple_of(r * GRID_W, GRID_W), GRID_W)
            krows = pl.ds(pl.multiple_of(r0 * GRID_W, GRID_W), wr * GRID_W)
            qst = _stack_heads(q_ref[qrows, :], head_v)
            scores.append(_dot_nt(qst, k_ref[krows, :]) + bias_ref[pat])
            slices.append((qrows, krows))
        probs = []
        for sc in scores:
            e = jnp.exp(sc - jnp.max(sc, axis=-1, keepdims=True))
            probs.append((e.astype(BF16), jnp.sum(e, axis=-1, keepdims=True)))
        pvs = [jnp.dot(e, v_ref[krows, :], preferred_element_type=F32) / l
               for (e, l), (_, krows) in zip(probs, slices)]
        for pv, (qrows, _) in zip(pvs, slices):
            o_ref[qrows, :] = _unstack_heads(pv, head_v, GRID_W).astype(o_ref.dtype)
        return carry

    lax.fori_loop(0, rows // NA_GROUP, group_body, 0)


def _neighbourhood(proj, rpb_tab, layer):
    b, s, _ = proj.shape
    bw = BRANCH_WIDTH
    n_tab = rpb_tab.shape[1]
    col = lambda cidx: pl.BlockSpec((None, s, bw), lambda i: (i, 0, cidx))
    return pl.pallas_call(
        _na_kernel,
        grid=(b,),
        in_specs=[
            col(5), col(6), col(7),
            _resident((None, n_tab, 128), lambda i: (layer, 0, 0)),
        ],
        out_specs=pl.BlockSpec((None, s, bw), lambda i: (i, 0, 0)),
        out_shape=jax.ShapeDtypeStruct((b, s, bw), BF16),
        scratch_shapes=[
            pltpu.VMEM((N_HEADS * (2 * NA_WIN_ROWS - 1) * GRID_W, 128), F32),
            pltpu.VMEM((NA_WIN_ROWS, N_HEADS * GRID_W, NA_WIN_ROWS * GRID_W), F32),
        ],
        compiler_params=_cparams(1),
        name="neighbourhood",
    )(proj, proj, proj, rpb_tab)


def _na_rpb_table(rpb):
    n_layers, n_heads, n_dr, n_dc = rpb.shape
    t = jnp.pad(rpb.astype(F32), ((0, 0), (0, 0), (0, 0), (0, GRID_W - n_dc)))
    t = jnp.roll(t, -(NA_WIN_COLS - 1), axis=-1)
    t = jnp.tile(t, (1, 1, 1, 128 // GRID_W)).reshape(n_layers, n_heads * n_dr, 128)
    return jnp.pad(t, ((0, 0), (0, -(n_heads * n_dr) % 8), (0, 0)))


def _memkv_kernel(m_ref, g_ref, w_ref, gk_ref, k_ref, v_ref):
    n_batch = k_ref.shape[0]
    n_mem = k_ref.shape[1] // N_HEADS
    bw = BRANCH_WIDTH
    head_v = _head_id((1, bw), 1)
    ones_bd = _head_block_ones()
    parts = [slice(i * n_mem, (i + 1) * n_mem) for i in range(n_batch)]
    mns = [_rms_rows(m_ref[r, :], g_ref[...]).astype(BF16) for r in parts]
    kvs = [jnp.dot(mn, w_ref[...], preferred_element_type=F32) for mn in mns]
    for i, kv in enumerate(kvs):
        kn = _head_rms(kv[:, :bw], gk_ref[...], ones_bd)
        k_ref[i] = _stack_heads(kn.astype(k_ref.dtype), head_v)
        v_ref[i] = _stack_heads(kv[:, bw:].astype(v_ref.dtype), head_v)


def _memkv(mem, norm_g, w_kv, gk):
    b, m, d = mem.shape
    n_layers = w_kv.shape[0]
    bw = BRANCH_WIDTH
    out = jax.ShapeDtypeStruct((n_layers, b, N_HEADS * m, bw), BF16)
    return pl.pallas_call(
        _memkv_kernel,
        grid=(n_layers,),
        in_specs=[
            _resident((b * m, d), lambda l: (0, 0)),
            pl.BlockSpec((None, 1, d), lambda l: (l, 0, 0)),
            pl.BlockSpec((None, d, 2 * bw), lambda l: (l, 0, 0)),
            pl.BlockSpec((None, 1, bw), lambda l: (l, 0, 0)),
        ],
        out_specs=[pl.BlockSpec((None, b, N_HEADS * m, bw), lambda l: (l, 0, 0, 0))] * 2,
        out_shape=[out, out],
        compiler_params=_cparams(1),
        name="memkv",
    )(mem.reshape(b * m, d), norm_g, w_kv, gk)


def _memattn_kernel(q_ref, k_ref, v_ref, o_ref):
    n_mem = k_ref.shape[0] // N_HEADS
    sub = MEM_SUB_ROWS
    head_v = _head_id((1, BRANCH_WIDTH), 1)

    def group_body(step, carry):
        tiles = [pl.ds(pl.multiple_of((step * MEM_GROUP + g) * sub, sub), sub) for g in range(MEM_GROUP)]
        scores = [_dot_nt(q_ref[t, :], k_ref[...]) for t in tiles]
        probs, inv_ls = [], []
        for sc in scores:
            es = []
            inv_l = jnp.zeros((sub, BRANCH_WIDTH), F32)
            for h in range(N_HEADS):
                seg = sc[:, h * n_mem:(h + 1) * n_mem]
                e = jnp.exp(seg - jnp.max(seg, axis=-1, keepdims=True))
                inv_l = jnp.where(head_v == h, 1.0 / jnp.sum(e, axis=-1, keepdims=True), inv_l)
                es.append(e.astype(BF16))
            probs.append(jnp.concatenate(es, axis=1))
            inv_ls.append(inv_l)
        pvs = [jnp.dot(p, v_ref[...], preferred_element_type=F32) for p in probs]
        for t, pv, inv_l in zip(tiles, pvs, inv_ls):
            o_ref[t, :] = (pv * inv_l).astype(o_ref.dtype)
        return carry

    lax.fori_loop(0, q_ref.shape[0] // (sub * MEM_GROUP), group_body, 0)


def _memattn(proj, mk, mv, layer):
    b, s, _ = proj.shape
    bw = BRANCH_WIDTH
    m = mk.shape[2]
    return pl.pallas_call(
        _memattn_kernel,
        grid=(b,),
        in_specs=[
            pl.BlockSpec((None, s, bw), lambda i: (i, 0, 8)),
            pl.BlockSpec((None, None, m, bw), lambda i: (layer, i, 0, 0)),
            pl.BlockSpec((None, None, m, bw), lambda i: (layer, i, 0, 0)),
        ],
        out_specs=pl.BlockSpec((None, s, bw), lambda i: (i, 0, 0)),
        out_shape=jax.ShapeDtypeStruct((b, s, bw), BF16),
        compiler_params=_cparams(1),
        name="memattn",
    )(proj, mk, mv)


def _merge_kernel(x_ref, g_ref, b0_ref, b1_ref, b2_ref, b3_ref, wg_ref, wb_ref, wo_ref, o_ref):
    d = x_ref.shape[-1]
    halves = _row_halves(x_ref.shape[0])
    hs = [_rms_rows(x_ref[r, :], g_ref[...]).astype(BF16) for r in halves]
    merged = [None] * len(halves)
    for n, br_ref in enumerate((b0_ref, b1_ref, b2_ref, b3_ref)):
        for i, r in enumerate(halves):
            gate = jax.nn.sigmoid(jnp.dot(hs[i], wg_ref[:, n * d:(n + 1) * d], preferred_element_type=F32))
            up = jnp.dot(br_ref[r, :], wb_ref[n], preferred_element_type=F32)
            merged[i] = gate * up if merged[i] is None else merged[i] + gate * up
    outs = [jnp.dot(m.astype(BF16), wo_ref[...], preferred_element_type=F32) for m in merged]
    for r, t in zip(halves, outs):
        o_ref[r, :] = x_ref[r, :] + t


def _merge(x2, norm_g, branches, w_gate, w_branch, w_out, layer, tm):
    t, d = x2.shape
    bw = BRANCH_WIDTH
    br_spec = pl.BlockSpec((tm, bw), lambda i: (i, 0))
    return pl.pallas_call(
        _merge_kernel,
        grid=(t // tm,),
        in_specs=[
            pl.BlockSpec((tm, d), lambda i: (i, 0)),
            _resident((None, 1, d), lambda i: (layer, 0, 0)),
            br_spec, br_spec, br_spec, br_spec,
            _resident((None, d, N_BRANCH * d), lambda i: (layer, 0, 0)),
            _resident((None, N_BRANCH, bw, d), lambda i: (layer, 0, 0, 0)),
            _resident((None, d, d), lambda i: (layer, 0, 0)),
        ],
        out_specs=pl.BlockSpec((tm, d), lambda i: (i, 0)),
        out_shape=jax.ShapeDtypeStruct((t, d), F32),
        compiler_params=_cparams(1),
        name="merge",
    )(x2, norm_g, *branches, w_gate, w_branch, w_out)


def _ffn_kernel(x_ref, g_ref, w1_ref, w2_ref, o_ref):
    ff = w2_ref.shape[0]
    halves = _row_halves(x_ref.shape[0])
    hn = [_rms_rows(x_ref[r, :], g_ref[...]).astype(BF16) for r in halves]
    ag = [jnp.dot(h, w1_ref[...], preferred_element_type=F32) for h in hn]
    u = [(t[:, :ff] * jax.nn.sigmoid(t[:, :ff]) * t[:, ff:]).astype(BF16) for t in ag]
    down = [jnp.dot(t, w2_ref[...], preferred_element_type=F32) for t in u]
    for r, t in zip(halves, down):
        o_ref[r, :] = x_ref[r, :] + t


def _ffn(x2, norm_g, w1, w2, layer, tm):
    t, d = x2.shape
    ff = w2.shape[1]
    return pl.pallas_call(
        _ffn_kernel,
        grid=(t // tm,),
        in_specs=[
            pl.BlockSpec((tm, d), lambda i: (i, 0)),
            _resident((None, 1, d), lambda i: (layer, 0, 0)),
            _resident((None, d, 2 * ff), lambda i: (layer, 0, 0)),
            _resident((None, ff, d), lambda i: (layer, 0, 0)),
        ],
        out_specs=pl.BlockSpec((tm, d), lambda i: (i, 0)),
        out_shape=jax.ShapeDtypeStruct((t, d), F32),
        compiler_params=_cparams(1),
        name="ffn",
    )(x2, norm_g, w1, w2)


def _block_diag(w):
    n_layers, g, c, e = w.shape
    eye = jnp.eye(g, dtype=w.dtype)
    return jnp.einsum('lgce,gk->lgcke', w, eye).reshape(n_layers, g * c, g * e)


def kernel(x, mem, norm_mix_g, norm_mem_g, w_in, w_gate, ret_decay_fwd, ret_decay_bwd, ret_norm_g, pool_w,
           pool_scale, na_q_norm_g, na_k_norm_g, na_rpb, mem_q_norm_g, mem_k_norm_g, w_mem_kv, w_branch, w_out,
           norm_ffn_g, w_ffn_in, w_ffn_out):
    b, s, d = x.shape
    n_layers = w_in.shape[0]
    bw = BRANCH_WIDTH

    half = HEAD_DIM // 2
    w_qk = w_in[:, :, :2 * bw].reshape(n_layers, d, 2, N_HEADS, 2, half)
    w_qk = w_qk.transpose(0, 1, 2, 4, 3, 5).reshape(n_layers, d, 2 * bw)
    w_in_b = jnp.concatenate([w_qk, w_in[:, :, 2 * bw:]], axis=-1).astype(BF16)
    w_gate_b = w_gate.astype(BF16)
    w_branch_b = w_branch.astype(BF16)
    w_out_b = w_out.astype(BF16)
    w_ffn_in_b = w_ffn_in.astype(BF16)
    w_ffn_out_b = w_ffn_out.astype(BF16)
    w_mem_kv_b = w_mem_kv.astype(BF16)
    pool_w_bd = _block_diag(pool_w).astype(BF16)

    row3 = lambda a: a.astype(F32).reshape(n_layers, 1, -1)
    tile_heads = lambda g: jnp.tile(g.astype(F32), (1, N_HEADS)).reshape(n_layers, 1, bw)
    norm_mix_g3, norm_mem_g3, norm_ffn_g3 = row3(norm_mix_g), row3(norm_mem_g), row3(norm_ffn_g)
    ret_norm_g3, pool_scale3 = row3(ret_norm_g), row3(pool_scale)
    na_gq, na_gk = tile_heads(na_q_norm_g), tile_heads(na_k_norm_g)
    mem_gq, mem_gk = tile_heads(mem_q_norm_g), tile_heads(mem_k_norm_g)
    dec = jnp.concatenate([ret_decay_fwd, ret_decay_bwd], axis=1).astype(F32)
    dec = jnp.broadcast_to(dec[:, :, None], (n_layers, 2 * N_HEADS, 128))
    rpb_tab = _na_rpb_table(na_rpb)

    inv = ROPE_THETA ** (-jnp.arange(half, dtype=F32) / half)
    ang = jnp.arange(s, dtype=F32)[:, None] * inv[None, :]
    cos = jnp.tile(jnp.cos(ang), (1, N_HEADS))
    sin = jnp.tile(jnp.sin(ang), (1, N_HEADS))

    mk_all, mv_all = _memkv(mem, norm_mem_g3, w_mem_kv_b, mem_gk)

    x2 = x.reshape(b * s, d)
    for layer in range(n_layers):
        proj = _inproj(x2, norm_mix_g3, w_in_b, cos, sin, na_gq, na_gk, mem_gq, layer,
                       tm=INPROJ_TILE_ROWS).reshape(b, s, -1)
        ret = _retention(proj, dec, ret_norm_g3, layer)
        pool = _pooling(proj, pool_w_bd, pool_scale3, layer)
        na = _neighbourhood(proj, rpb_tab, layer)
        mo = _memattn(proj, mk_all, mv_all, layer)
        branches = [t.reshape(b * s, bw) for t in (ret, pool, na, mo)]
        x2 = _merge(x2, norm_mix_g3, branches, w_gate_b, w_branch_b, w_out_b, layer, tm=MERGE_TILE_ROWS)
        x2 = _ffn(x2, norm_ffn_g3, w_ffn_in_b, w_ffn_out_b, layer, tm=FFN_TILE_ROWS)
    return x2.reshape(b, s, d)
```

```python
import functools

import numpy as np
import jax
import jax.numpy as jnp
from jax import lax
from jax.experimental import pallas as pl
from jax.experimental.pallas import tpu as pltpu

F32 = jnp.float32
BF16 = jnp.bfloat16

HEAD_DIM = 64
BRANCH_WIDTH = 256
N_HEADS = BRANCH_WIDTH // HEAD_DIM
N_BRANCH = 4
GRID_W = 64
RET_CHUNK = 128
RET_GROUP = 8
NA_GROUP = 8
MEM_SUB_ROWS = 128
MEM_GROUP = 8
ROPE_THETA = 10000.0
POOL_WINDOWS = (2, 4, 8, 16)
POOL_PAD = 16
NA_WIN_ROWS = 8
NA_WIN_COLS = 16
NEG_INF = -1e30
EPS = 1e-6
QK_SCALE = HEAD_DIM ** -0.5

VMEM_LIMIT_BYTES = 56 * 1024 * 1024

INPROJ_TILE_ROWS = 1024
MERGE_TILE_ROWS = 1024
DENSE_PART_ROWS = 256
FFN_TILE_ROWS = 512


def _cparams(n_grid_dims):
    return pltpu.CompilerParams(
        dimension_semantics=("arbitrary",) * n_grid_dims,
        vmem_limit_bytes=VMEM_LIMIT_BYTES,
    )


def _resident(block_shape, index_map):
    return pl.BlockSpec(block_shape, index_map, pipeline_mode=pl.Buffered(1))


def _rms_rows(x, g):
    ms = jnp.mean(x * x, axis=-1, keepdims=True)
    return x * lax.rsqrt(ms + EPS) * g


def _head_id(shape, axis):
    return lax.shift_right_logical(lax.broadcasted_iota(jnp.int32, shape, axis), 6)


def _head_block_ones():
    shape = (BRANCH_WIDTH, BRANCH_WIDTH)
    return (_head_id(shape, 0) == _head_id(shape, 1)).astype(BF16)


def _head_sumsq(t, ones_bd):
    return jnp.dot((t * t).astype(BF16), ones_bd, preferred_element_type=F32)


def _head_rms(t, g, ones_bd):
    ms = _head_sumsq(t, ones_bd) * (1.0 / HEAD_DIM)
    return t * lax.rsqrt(ms + EPS) * g


def _row_halves(n):
    return [slice(0, n // 2), slice(n // 2, n)]


def _row_pairs(n):
    parts = [slice(i, i + DENSE_PART_ROWS) for i in range(0, n, DENSE_PART_ROWS)]
    return [parts[i:i + 2] for i in range(0, len(parts), 2)]


def _log_sigmoid(x):
    return jnp.minimum(x, 0.0) - jnp.log1p(jnp.exp(-jnp.abs(x)))


def _dot_nt(a, b):
    return lax.dot_general(a, b, (((1,), (1,)), ((), ())), preferred_element_type=F32)


def _dot_tn(a, b):
    return lax.dot_general(a, b, (((0,), (0,)), ((), ())), preferred_element_type=F32)


def _inproj_kernel(x_ref, g_ref, w_ref, cos_ref, sin_ref, gnq_ref, gnk_ref, gmq_ref, o_ref):
    bw = BRANCH_WIDTH
    half = bw // 2
    ones_bd = _head_block_ones()
    halves = _row_halves(x_ref.shape[0])
    hs = [_rms_rows(x_ref[r, :], g_ref[...]).astype(BF16) for r in halves]
    ps = [jnp.dot(h, w_ref[...], preferred_element_type=F32) for h in hs]
    for r, p in zip(halves, ps):
        cos, sin = cos_ref[r, :], sin_ref[r, :]

        def store(c0, val):
            o_ref[r, c0:c0 + val.shape[1]] = val.astype(o_ref.dtype)

        for c0, scale in ((0, QK_SCALE), (bw, None)):
            a, b = p[:, c0:c0 + half], p[:, c0 + half:c0 + bw]
            lo, hi = a * cos - b * sin, a * sin + b * cos
            store(c0, lo if scale is None else lo * scale)
            store(c0 + half, hi if scale is None else hi * scale)
        store(2 * bw, p[:, 2 * bw:5 * bw])
        store(5 * bw, _head_rms(p[:, 5 * bw:6 * bw], gnq_ref[...], ones_bd) * QK_SCALE)
        store(6 * bw, _head_rms(p[:, 6 * bw:7 * bw], gnk_ref[...], ones_bd))
        store(7 * bw, p[:, 7 * bw:8 * bw])
        store(8 * bw, _head_rms(p[:, 8 * bw:9 * bw], gmq_ref[...], ones_bd) * QK_SCALE)


def _inproj(x2, norm_g, w_in, cos, sin, na_gq, na_gk, mem_gq, layer, tm):
    t, d = x2.shape
    n_out = w_in.shape[-1]
    bw = BRANCH_WIDTH
    seq_tiles = cos.shape[0] // tm
    head_gain = _resident((None, 1, bw), lambda i: (layer, 0, 0))
    return pl.pallas_call(
        _inproj_kernel,
        grid=(t // tm,),
        in_specs=[
            pl.BlockSpec((tm, d), lambda i: (i, 0)),
            _resident((None, 1, d), lambda i: (layer, 0, 0)),
            _resident((d, n_out), lambda i: (0, 0)),
            pl.BlockSpec((tm, bw // 2), lambda i: (i % seq_tiles, 0)),
            pl.BlockSpec((tm, bw // 2), lambda i: (i % seq_tiles, 0)),
            head_gain, head_gain, head_gain,
        ],
        out_specs=pl.BlockSpec((tm, n_out), lambda i: (i, 0)),
        out_shape=jax.ShapeDtypeStruct((t, n_out), BF16),
        compiler_params=_cparams(1),
        name="inproj",
    )(x2, norm_g, w_in, cos, sin, na_gq, na_gk, mem_gq)


def _ret_kernel(q_ref, k_ref, v_ref, g_ref, dec_ref, gn_ref, o_ref, sf_ref, sb_ref, st_ref, dm_ref):
    c = RET_CHUNK
    n_chunks = q_ref.shape[0] // c
    half = BRANCH_WIDTH // 2

    lane = lax.broadcasted_iota(jnp.int32, (1, BRANCH_WIDTH), 1)
    head_qk = lax.shift_right_logical(jnp.bitwise_and(lane, half - 1), 5)
    head_v = lax.shift_right_logical(lane, 6)

    lg = _log_sigmoid(dec_ref[...])
    lg2 = jnp.concatenate([lg, lg], axis=1)

    def per_lane(head_vec, row0):
        out = jnp.zeros((1, BRANCH_WIDTH), F32)
        for h in range(N_HEADS):
            out = jnp.where(head_vec == h, lg2[row0 + h:row0 + h + 1, :], out)
        return out

    lgf_qk, lgb_qk = per_lane(head_qk, 0), per_lane(head_qk, N_HEADS)
    lgf_v, lgb_v = per_lane(head_v, 0), per_lane(head_v, N_HEADS)

    idx = lax.broadcasted_iota(jnp.int32, (c, 1), 0).astype(F32)
    kdec_f = jnp.exp((c - 1 - idx) * lgf_qk)
    qdec_f = jnp.exp((idx + 1) * lgf_qk)
    kdec_b = jnp.exp(idx * lgb_qk)
    qdec_b = jnp.exp((c - idx) * lgb_qk)
    cdec_f = jnp.exp(c * lgf_v)
    cdec_b = jnp.exp(c * lgb_v)

    diff = (lax.broadcasted_iota(jnp.int32, (c, c), 0)
            - lax.broadcasted_iota(jnp.int32, (c, c), 1)).astype(F32)
    for h in range(N_HEADS):
        lf = lg[h:h + 1, :]
        lb = lg[N_HEADS + h:N_HEADS + h + 1, :]
        dm_ref[:, h * c:(h + 1) * c] = jnp.where(diff >= 0, jnp.exp(jnp.maximum(diff, 0.0) * lf),
                                                 jnp.exp(jnp.maximum(-diff, 0.0) * lb))

    row_head = lax.shift_right_logical(
        jnp.bitwise_and(lax.broadcasted_iota(jnp.int32, (BRANCH_WIDTH, BRANCH_WIDTH), 0), half - 1), 5)
    col_head = _head_id((BRANCH_WIDTH, BRANCH_WIDTH), 1)
    same_head = row_head == col_head

    group = RET_GROUP
    chunk_rows = lambda j: pl.ds(pl.multiple_of(j * c, c), c)

    def sweep(kdec, cdec, states_ref, chunk_of):
        st_ref[...] = jnp.zeros_like(st_ref)

        def body(step, carry):
            chunks = [chunk_of(step * group + g) for g in range(group)]
            kvs = [_dot_tn((k_ref[chunk_rows(j), :].astype(F32) * kdec).astype(BF16), v_ref[chunk_rows(j), :])
                   for j in chunks]
            state = st_ref[...]
            for j, kv in zip(chunks, kvs):
                states_ref[j] = state.astype(BF16)
                state = cdec * state + jnp.where(same_head, kv, 0.0)
            st_ref[...] = state
            return carry

        lax.fori_loop(0, n_chunks // group, body, 0)

    sweep(kdec_f, cdec_f, sf_ref, lambda i: i)
    sweep(kdec_b, cdec_b, sb_ref, lambda i: n_chunks - 1 - i)

    ones_bd = _head_block_ones()
    gn = gn_ref[...]

    def out_body(step, carry):
        chunks = [step * group + g for g in range(group)]
        accs = []
        for j in chunks:
            rows = chunk_rows(j)
            qb = q_ref[rows, :]
            q = qb.astype(F32)
            cross = (jnp.dot((q * qdec_f).astype(BF16), sf_ref[j], preferred_element_type=F32)
                     + jnp.dot((q * qdec_b).astype(BF16), sb_ref[j], preferred_element_type=F32))
            scores = _dot_nt(qb, _stack_heads(k_ref[rows, :], head_qk))
            accs.append((cross, scores))
        accs = [cross + jnp.dot((scores * dm_ref[...]).astype(BF16), _stack_heads(v_ref[chunk_rows(j), :], head_v),
                                preferred_element_type=F32)
                for j, (cross, scores) in zip(chunks, accs)]
        sumsq = [_head_sumsq(acc, ones_bd) for acc in accs]
        for j, acc, ss in zip(chunks, accs, sumsq):
            rows = chunk_rows(j)
            y = acc * lax.rsqrt(ss * (1.0 / HEAD_DIM) + EPS) * gn
            gate = g_ref[rows, :].astype(F32)
            o_ref[rows, :] = (y * (gate * jax.nn.sigmoid(gate))).astype(o_ref.dtype)
        return carry

    lax.fori_loop(0, n_chunks // group, out_body, 0)


def _retention(proj, dec, gn, layer):
    b, s, _ = proj.shape
    bw = BRANCH_WIDTH
    n_chunks = s // RET_CHUNK
    col = lambda cidx: pl.BlockSpec((None, s, bw), lambda i: (i, 0, cidx))
    return pl.pallas_call(
        _ret_kernel,
        grid=(b,),
        in_specs=[
            col(0), col(1), col(2), col(3),
            _resident((None, 2 * N_HEADS, 128), lambda i: (layer, 0, 0)),
            _resident((None, 1, bw), lambda i: (layer, 0, 0)),
        ],
        out_specs=pl.BlockSpec((None, s, bw), lambda i: (i, 0, 0)),
        out_shape=jax.ShapeDtypeStruct((b, s, bw), BF16),
        scratch_shapes=[
            pltpu.VMEM((n_chunks, bw, bw), BF16),
            pltpu.VMEM((n_chunks, bw, bw), BF16),
            pltpu.VMEM((bw, bw), F32),
            pltpu.VMEM((RET_CHUNK, N_HEADS * RET_CHUNK), F32),
        ],
        compiler_params=_cparams(1),
        name="retention",
    )(proj, proj, proj, proj, dec, gn)


def _pool_kernel(v_ref, w_ref, sc_ref, o_ref, pad_ref):
    s = v_ref.shape[0]
    pad = POOL_PAD
    tile = 512
    ext = tile + 2 * pad

    pad_ref[0:pad, :] = jnp.zeros((pad, BRANCH_WIDTH), F32)
    pad_ref[pad + s:pad + s + pad, :] = jnp.zeros((pad, BRANCH_WIDTH), F32)
    pad_ref[pad:pad + s, :] = v_ref[...].astype(F32)

    grp = _head_id((1, BRANCH_WIDTH), 1)
    halfw = jnp.where(grp == 0, POOL_WINDOWS[0] // 2,
                      jnp.where(grp == 1, POOL_WINDOWS[1] // 2,
                                jnp.where(grp == 2, POOL_WINDOWS[2] // 2, POOL_WINDOWS[3] // 2)))

    half_lanes = BRANCH_WIDTH // 2
    grp_lo, grp_hi = grp[:, :half_lanes], grp[:, half_lanes:]
    for ci in range(s // tile):
        p = pad_ref[ci * tile:ci * tile + ext, :]
        s2 = p + pltpu.roll(p, 1, 0)
        s4 = pltpu.roll(s2, 1, 0) + pltpu.roll(s2, ext - 1, 0)
        s4_hi = s4[:, half_lanes:]
        s8 = pltpu.roll(s4_hi, 2, 0) + pltpu.roll(s4_hi, ext - 2, 0)
        s16 = pltpu.roll(s8, 4, 0) + pltpu.roll(s8, ext - 4, 0)
        win = jnp.concatenate([jnp.where(grp_lo == 0, s2[:, :half_lanes], s4[:, :half_lanes]),
                               jnp.where(grp_hi == 2, s8, s16)], axis=1)
        win = win[pad:pad + tile, :]
        if 0 < ci < s // tile - 1:
            pooled = win * (0.5 / halfw.astype(F32)) - p[pad:pad + tile, :]
        else:
            t = ci * tile + lax.broadcasted_iota(jnp.int32, (tile, 1), 0)
            count = (jnp.minimum(t + halfw, s) - jnp.maximum(t - halfw, 0)).astype(F32)
            pooled = win / count - p[pad:pad + tile, :]
        mixed = jnp.dot(pooled.astype(BF16), w_ref[...], preferred_element_type=F32) * sc_ref[...]
        o_ref[ci * tile:(ci + 1) * tile, :] = mixed.astype(o_ref.dtype)


def _pooling(proj, w_bd, scale, layer):
    b, s, _ = proj.shape
    bw = BRANCH_WIDTH
    return pl.pallas_call(
        _pool_kernel,
        grid=(b,),
        in_specs=[
            pl.BlockSpec((None, s, bw), lambda i: (i, 0, 4)),
            _resident((None, bw, bw), lambda i: (layer, 0, 0)),
            _resident((None, 1, bw), lambda i: (layer, 0, 0)),
        ],
        out_specs=pl.BlockSpec((None, s, bw), lambda i: (i, 0, 0)),
        out_shape=jax.ShapeDtypeStruct((b, s, bw), BF16),
        scratch_shapes=[pltpu.VMEM((s + 2 * POOL_PAD, bw), F32)],
        compiler_params=_cparams(1),
        name="pooling",
    )(proj, w_bd, scale)


def _stack_heads(t, head_vec):
    return jnp.concatenate([jnp.where(head_vec == h, t, jnp.zeros_like(t)) for h in range(N_HEADS)], axis=0)


def _unstack_heads(t, head_vec, n):
    out = jnp.where(head_vec == 0, t[0:n], 0.0)
    for h in range(1, N_HEADS):
        out = out + jnp.where(head_vec == h, t[h * n:(h + 1) * n], 0.0)
    return out


def _na_pattern_first_offset(pat, rows):
    wr = NA_WIN_ROWS
    query_row = pat if pat <= wr // 2 else rows - wr + pat
    first_key_row = min(max(query_row - wr // 2, 0), rows - wr)
    return first_key_row - query_row + wr - 1


def _na_build_bias(tab_ref, toep_ref, bias_ref, rows):
    wr, wc = NA_WIN_ROWS, NA_WIN_COLS
    n_rel = N_HEADS * (2 * wr - 1)
    x = jnp.concatenate([jnp.broadcast_to(tab_ref[i:i + 1, :], (GRID_W, 128)) for i in range(n_rel)], axis=0)
    qcol = jnp.bitwise_and(lax.broadcasted_iota(jnp.int32, x.shape, 0), GRID_W - 1)
    for bit in range(GRID_W.bit_length() - 1):
        x = jnp.where(jnp.bitwise_and(qcol, 1 << bit) != 0, pltpu.roll(x, 1 << bit, 1), x)
    toep_ref[...] = x

    lane = lax.broadcasted_iota(jnp.int32, (GRID_W, 128), 1)
    kcol = jnp.bitwise_and(lane, GRID_W - 1)
    qwin = jnp.clip(lax.broadcasted_iota(jnp.int32, (GRID_W, 128), 0) - wc // 2, 0, GRID_W - wc)
    in_window = (kcol >= qwin) & (kcol < qwin + wc)
    first_half = lane < GRID_W
    for pat in range(wr):
        dr0 = _na_pattern_first_offset(pat, rows)
        for h in range(N_HEADS):
            for pair in range(wr // 2):
                i0 = h * (2 * wr - 1) + dr0 + 2 * pair
                t = jnp.where(first_half, toep_ref[i0 * GRID_W:(i0 + 1) * GRID_W, :],
                              toep_ref[(i0 + 1) * GRID_W:(i0 + 2) * GRID_W, :])
                bias_ref[pat, h * GRID_W:(h + 1) * GRID_W, pair * 128:(pair + 1) * 128] = (
                    jnp.where(in_window, t, NEG_INF))


def _na_kernel(q_ref, k_ref, v_ref, tab_ref, o_ref, toep_ref, bias_ref):
    s = q_ref.shape[0]
    rows = s // GRID_W
    wr = NA_WIN_ROWS
    head_v = _head_id((1, BRANCH_WIDTH), 1)

    @pl.when(pl.program_id(0) == 0)
    def _():
        _na_build_bias(tab_ref, toep_ref, bias_ref, rows)

    def group_body(step, carry):
        slices, scores = [], []
        for g in range(NA_GROUP):
            r = step * NA_GROUP + g
            r0 = jnp.clip(r - wr // 2, 0, rows - wr)
            pat = jnp.where(r < wr // 2, r, jnp.where(r > rows - wr // 2, r - (rows - wr), wr // 2))
            qrows = pl.ds(pl.multiple_of(r * GRID_W, GRID_W), GRID_W)
            krows = pl.ds(pl.multiple_of(r0 * GRID_W, GRID_W), wr * GRID_W)
            qst = _stack_heads(q_ref[qrows, :], head_v)
            scores.append(_dot_nt(qst, k_ref[krows, :]) + bias_ref[pat])
            slices.append((qrows, krows))
        probs = []
        for sc in scores:
            e = jnp.exp(sc - jnp.max(sc, axis=-1, keepdims=True))
            probs.append((e.astype(BF16), jnp.sum(e, axis=-1, keepdims=True)))
        pvs = [jnp.dot(e, v_ref[krows, :], preferred_element_type=F32) / l
               for (e, l), (_, krows) in zip(probs, slices)]
        for pv, (qrows, _) in zip(pvs, slices):
            o_ref[qrows, :] = _unstack_heads(pv, head_v, GRID_W).astype(o_ref.dtype)
        return carry

    lax.fori_loop(0, rows // NA_GROUP, group_body, 0)


def _neighbourhood(proj, rpb_tab, layer):
    b, s, _ = proj.shape
    bw = BRANCH_WIDTH
    n_tab = rpb_tab.shape[1]
    col = lambda cidx: pl.BlockSpec((None, s, bw), lambda i: (i, 0, cidx))
    return pl.pallas_call(
        _na_kernel,
        grid=(b,),
        in_specs=[
            col(5), col(6), col(7),
            _resident((None, n_tab, 128), lambda i: (layer, 0, 0)),
        ],
        out_specs=pl.BlockSpec((None, s, bw), lambda i: (i, 0, 0)),
        out_shape=jax.ShapeDtypeStruct((b, s, bw), BF16),
        scratch_shapes=[
            pltpu.VMEM((N_HEADS * (2 * NA_WIN_ROWS - 1) * GRID_W, 128), F32),
            pltpu.VMEM((NA_WIN_ROWS, N_HEADS * GRID_W, NA_WIN_ROWS * GRID_W), F32),
        ],
        compiler_params=_cparams(1),
        name="neighbourhood",
    )(proj, proj, proj, rpb_tab)


def _na_rpb_table(rpb):
    n_layers, n_heads, n_dr, n_dc = rpb.shape
    t = jnp.pad(rpb.astype(F32), ((0, 0), (0, 0), (0, 0), (0, GRID_W - n_dc)))
    t = jnp.roll(t, -(NA_WIN_COLS - 1), axis=-1)
    t = jnp.tile(t, (1, 1, 1, 128 // GRID_W)).reshape(n_layers, n_heads * n_dr, 128)
    return jnp.pad(t, ((0, 0), (0, -(n_heads * n_dr) % 8), (0, 0)))


def _memkv_kernel(m_ref, g_ref, w_ref, gk_ref, k_ref, v_ref):
    n_batch = k_ref.shape[0]
    n_mem = k_ref.shape[1] // N_HEADS
    bw = BRANCH_WIDTH
    head_v = _head_id((1, bw), 1)
    ones_bd = _head_block_ones()
    parts = [slice(i * n_mem, (i + 1) * n_mem) for i in range(n_batch)]
    mns = [_rms_rows(m_ref[r, :], g_ref[...]).astype(BF16) for r in parts]
    kvs = [jnp.dot(mn, w_ref[...], preferred_element_type=F32) for mn in mns]
    for i, kv in enumerate(kvs):
        kn = _head_rms(kv[:, :bw], gk_ref[...], ones_bd)
        k_ref[i] = _stack_heads(kn.astype(k_ref.dtype), head_v)
        v_ref[i] = _stack_heads(kv[:, bw:].astype(v_ref.dtype), head_v)


def _memkv(mem, norm_g, w_kv, gk):
    b, m, d = mem.shape
    n_layers = w_kv.shape[0]
    bw = BRANCH_WIDTH
    out = jax.ShapeDtypeStruct((n_layers, b, N_HEADS * m, bw), BF16)
    return pl.pallas_call(
        _memkv_kernel,
        grid=(n_layers,),
        in_specs=[
            _resident((b * m, d), lambda l: (0, 0)),
            pl.BlockSpec((None, 1, d), lambda l: (l, 0, 0)),
            pl.BlockSpec((None, d, 2 * bw), lambda l: (l, 0, 0)),
            pl.BlockSpec((None, 1, bw), lambda l: (l, 0, 0)),
        ],
        out_specs=[pl.BlockSpec((None, b, N_HEADS * m, bw), lambda l: (l, 0, 0, 0))] * 2,
        out_shape=[out, out],
        compiler_params=_cparams(1),
        name="memkv",
    )(mem.reshape(b * m, d), norm_g, w_kv, gk)


def _memattn_kernel(q_ref, k_ref, v_ref, o_ref):
    n_mem = k_ref.shape[0] // N_HEADS
    sub = MEM_SUB_ROWS
    head_v = _head_id((1, BRANCH_WIDTH), 1)

    def group_body(step, carry):
        tiles = [pl.ds(pl.multiple_of((step * MEM_GROUP + g) * sub, sub), sub) for g in range(MEM_GROUP)]
        scores = [_dot_nt(q_ref[t, :], k_ref[...]) for t in tiles]
        probs, inv_ls = [], []
        for sc in scores:
            es = []
            inv_l = jnp.zeros((sub, BRANCH_WIDTH), F32)
            for h in range(N_HEADS):
                seg = sc[:, h * n_mem:(h + 1) * n_mem]
                e = jnp.exp(seg - jnp.max(seg, axis=-1, keepdims=True))
                inv_l = jnp.where(head_v == h, 1.0 / jnp.sum(e, axis=-1, keepdims=True), inv_l)
                es.append(e.astype(BF16))
            probs.append(jnp.concatenate(es, axis=1))
            inv_ls.append(inv_l)
        pvs = [jnp.dot(p, v_ref[...], preferred_element_type=F32) for p in probs]
        for t, pv, inv_l in zip(tiles, pvs, inv_ls):
            o_ref[t, :] = (pv * inv_l).astype(o_ref.dtype)
        return carry

    lax.fori_loop(0, q_ref.shape[0] // (sub * MEM_GROUP), group_body, 0)


def _memattn(proj, mk, mv, layer):
    b, s, _ = proj.shape
    bw = BRANCH_WIDTH
    m = mk.shape[2]
    return pl.pallas_call(
        _memattn_kernel,
        grid=(b,),
        in_specs=[
            pl.BlockSpec((None, s, bw), lambda i: (i, 0, 8)),
            pl.BlockSpec((None, None, m, bw), lambda i: (layer, i, 0, 0)),
            pl.BlockSpec((None, None, m, bw), lambda i: (layer, i, 0, 0)),
        ],
        out_specs=pl.BlockSpec((None, s, bw), lambda i: (i, 0, 0)),
        out_shape=jax.ShapeDtypeStruct((b, s, bw), BF16),
        compiler_params=_cparams(1),
        name="memattn",
    )(proj, mk, mv)


def _merge_kernel(x_ref, g_ref, b0_ref, b1_ref, b2_ref, b3_ref, wg_ref, wb_ref, wo_ref, o_ref):
    d = x_ref.shape[-1]
    for halves in _row_pairs(x_ref.shape[0]):
        hs = [_rms_rows(x_ref[r, :], g_ref[...]).astype(BF16) for r in halves]
        merged = [None] * len(halves)
        for n, br_ref in enumerate((b0_ref, b1_ref, b2_ref, b3_ref)):
            for i, r in enumerate(halves):
                gate = jax.nn.sigmoid(jnp.dot(hs[i], wg_ref[:, n * d:(n + 1) * d], preferred_element_type=F32))
                up = jnp.dot(br_ref[r, :], wb_ref[n * BRANCH_WIDTH:(n + 1) * BRANCH_WIDTH, :],
                             preferred_element_type=F32)
                merged[i] = gate * up if merged[i] is None else merged[i] + gate * up
        outs = [jnp.dot(m.astype(BF16), wo_ref[...], preferred_element_type=F32) for m in merged]
        for r, t in zip(halves, outs):
            o_ref[r, :] = x_ref[r, :] + t


def _merge(x2, norm_g, branches, w_gate, w_branch, w_out, layer, tm):
    t, d = x2.shape
    bw = BRANCH_WIDTH
    br_spec = pl.BlockSpec((tm, bw), lambda i: (i, 0))
    return pl.pallas_call(
        _merge_kernel,
        grid=(t // tm,),
        in_specs=[
            pl.BlockSpec((tm, d), lambda i: (i, 0)),
            _resident((None, 1, d), lambda i: (layer, 0, 0)),
            br_spec, br_spec, br_spec, br_spec,
            _resident((d, N_BRANCH * d), lambda i: (0, 0)),
            _resident((N_BRANCH * bw, d), lambda i: (0, 0)),
            _resident((d, d), lambda i: (0, 0)),
        ],
        out_specs=pl.BlockSpec((tm, d), lambda i: (i, 0)),
        out_shape=jax.ShapeDtypeStruct((t, d), F32),
        compiler_params=_cparams(1),
        name="merge",
    )(x2, norm_g, *branches, w_gate, w_branch, w_out)


def _ffn_kernel(*refs, n_cast):
    x_ref, g_ref, w1_ref, w2_ref = refs[:4]
    cast_src = refs[4:4 + n_cast]
    o_ref = refs[4 + n_cast]
    cast_dst = refs[5 + n_cast:]
    ff = w2_ref.shape[0]
    halves = _row_halves(x_ref.shape[0])
    hn = [_rms_rows(x_ref[r, :], g_ref[...]).astype(BF16) for r in halves]
    ag = [jnp.dot(h, w1_ref[...], preferred_element_type=F32) for h in hn]
    u = [(t[:, :ff] * jax.nn.sigmoid(t[:, :ff]) * t[:, ff:]).astype(BF16) for t in ag]
    down = [jnp.dot(t, w2_ref[...], preferred_element_type=F32) for t in u]
    for r, t in zip(halves, down):
        o_ref[r, :] = x_ref[r, :] + t

    for src, dst in zip(cast_src, cast_dst):
        dst[...] = src[...].astype(dst.dtype)


BF16_SUBLANES = 16


def _cast_slabs(rows, n_steps):
    for steps in (n_steps, n_steps // 2):
        if rows % (steps * BF16_SUBLANES) == 0:
            return rows // steps, steps
    raise ValueError(f"cannot split {rows} weight rows over {n_steps} grid steps")


def _ffn(x2, norm_g, w1, w2, layer, tm, cast_weights=()):
    t, d = x2.shape
    ff = w2.shape[0]
    n_steps = t // tm
    slabs = [_cast_slabs(w.shape[1], n_steps) for w in cast_weights]
    slab_index = lambda steps: (lambda i: (jnp.minimum(i, steps - 1), 0))
    src_specs = [pl.BlockSpec((None, rows, w.shape[2]), lambda i, steps=steps: (layer + 1, jnp.minimum(i, steps - 1), 0))
                 for w, (rows, steps) in zip(cast_weights, slabs)]
    cast_specs = [pl.BlockSpec((rows, w.shape[2]), slab_index(steps)) for w, (rows, steps) in zip(cast_weights, slabs)]
    outs = pl.pallas_call(
        functools.partial(_ffn_kernel, n_cast=len(cast_weights)),
        grid=(n_steps,),
        in_specs=[
            pl.BlockSpec((tm, d), lambda i: (i, 0)),
            _resident((None, 1, d), lambda i: (layer, 0, 0)),
            _resident((d, 2 * ff), lambda i: (0, 0)),
            _resident((ff, d), lambda i: (0, 0)),
            *src_specs,
        ],
        out_specs=[pl.BlockSpec((tm, d), lambda i: (i, 0)), *cast_specs],
        out_shape=[jax.ShapeDtypeStruct((t, d), F32),
                   *[jax.ShapeDtypeStruct(w.shape[1:], BF16) for w in cast_weights]],
        compiler_params=_cparams(1),
        name="ffn",
    )(x2, norm_g, w1, w2, *cast_weights)
    return outs[0], list(outs[1:])


def _block_diag(w):
    n_layers, g, c, e = w.shape
    eye = jnp.eye(g, dtype=w.dtype)
    return jnp.einsum('lgce,gk->lgcke', w, eye).reshape(n_layers, g * c, g * e)


def kernel(x, mem, norm_mix_g, norm_mem_g, w_in, w_gate, ret_decay_fwd, ret_decay_bwd, ret_norm_g, pool_w,
           pool_scale, na_q_norm_g, na_k_norm_g, na_rpb, mem_q_norm_g, mem_k_norm_g, w_mem_kv, w_branch, w_out,
           norm_ffn_g, w_ffn_in, w_ffn_out):
    b, s, d = x.shape
    n_layers = w_in.shape[0]
    bw = BRANCH_WIDTH

    half = HEAD_DIM // 2

    dense_f32 = [w_in, w_gate, w_branch.reshape(n_layers, N_BRANCH * bw, d), w_out, w_ffn_in, w_ffn_out]

    def rotary_split_cols(w):
        w_qk = w[:, :2 * bw].reshape(d, 2, N_HEADS, 2, half).transpose(0, 1, 3, 2, 4).reshape(d, 2 * bw)
        return jnp.concatenate([w_qk, w[:, 2 * bw:]], axis=-1)

    dense_b = [w[0].astype(BF16) for w in dense_f32]
    w_mem_kv_b = w_mem_kv.astype(BF16)
    pool_w_bd = _block_diag(pool_w).astype(BF16)

    row3 = lambda a: a.astype(F32).reshape(n_layers, 1, -1)
    tile_heads = lambda g: jnp.tile(g.astype(F32), (1, N_HEADS)).reshape(n_layers, 1, bw)
    norm_mix_g3, norm_mem_g3, norm_ffn_g3 = row3(norm_mix_g), row3(norm_mem_g), row3(norm_ffn_g)
    ret_norm_g3, pool_scale3 = row3(ret_norm_g), row3(pool_scale)
    na_gq, na_gk = tile_heads(na_q_norm_g), tile_heads(na_k_norm_g)
    mem_gq, mem_gk = tile_heads(mem_q_norm_g), tile_heads(mem_k_norm_g)
    dec = jnp.concatenate([ret_decay_fwd, ret_decay_bwd], axis=1).astype(F32)
    dec = jnp.broadcast_to(dec[:, :, None], (n_layers, 2 * N_HEADS, 128))
    rpb_tab = _na_rpb_table(na_rpb)

    inv = ROPE_THETA ** (-jnp.arange(half, dtype=F32) / half)
    ang = jnp.arange(s, dtype=F32)[:, None] * inv[None, :]
    cos = jnp.tile(jnp.cos(ang), (1, N_HEADS))
    sin = jnp.tile(jnp.sin(ang), (1, N_HEADS))

    mk_all, mv_all = _memkv(mem, norm_mem_g3, w_mem_kv_b, mem_gk)

    x2 = x.reshape(b * s, d)
    for layer in range(n_layers):
        w_in_b, w_gate_b, w_branch_b, w_out_b, w_ffn_in_b, w_ffn_out_b = dense_b
        proj = _inproj(x2, norm_mix_g3, rotary_split_cols(w_in_b), cos, sin, na_gq, na_gk, mem_gq, layer,
                       tm=INPROJ_TILE_ROWS).reshape(b, s, -1)
        ret = _retention(proj, dec, ret_norm_g3, layer)
        pool = _pooling(proj, pool_w_bd, pool_scale3, layer)
        na = _neighbourhood(proj, rpb_tab, layer)
        mo = _memattn(proj, mk_all, mv_all, layer)
        branches = [t.reshape(b * s, bw) for t in (ret, pool, na, mo)]
        x2 = _merge(x2, norm_mix_g3, branches, w_gate_b, w_branch_b, w_out_b, layer, tm=MERGE_TILE_ROWS)
        later = dense_f32 if layer + 1 < n_layers else []
        x2, dense_b = _ffn(x2, norm_ffn_g3, w_ffn_in_b, w_ffn_out_b, layer, tm=FFN_TILE_ROWS, cast_weights=later)
    return x2.reshape(b, s, d)
```

```python
import functools

import jax
import jax.numpy as jnp
from jax import lax
from jax.experimental import pallas as pl
from jax.experimental.pallas import tpu as pltpu

F32 = jnp.float32
BF16 = jnp.bfloat16

HEAD_DIM = 64
HEAD_DIM_LOG2 = HEAD_DIM.bit_length() - 1
BRANCH_WIDTH = 256
N_HEADS = BRANCH_WIDTH // HEAD_DIM
N_BRANCH = 4
GRID_W = 64
RET_CHUNK = 128
RET_GROUP = 16
NA_GROUP = 32
MEM_SUB_ROWS = 128
MEM_GROUP = 16
ROPE_THETA = 10000.0
POOL_WINDOWS = (2, 4, 8, 16)
POOL_PAD = 16
NA_WIN_ROWS = 8
NA_WIN_COLS = 16
NEG_INF = -1e30
EPS = 1e-6
QK_SCALE = HEAD_DIM ** -0.5

LANES = 128
VMEM_LIMIT_BYTES = 56 * 1024 * 1024

INPROJ_TILE_ROWS = 1024
MERGE_TILE_ROWS = 1024
DENSE_PART_ROWS = 256
FFN_TILE_ROWS = 512


def _cparams(n_grid_dims):
    return pltpu.CompilerParams(
        dimension_semantics=("arbitrary",) * n_grid_dims,
        vmem_limit_bytes=VMEM_LIMIT_BYTES,
    )


def _resident(block_shape, index_map):
    return pl.BlockSpec(block_shape, index_map, pipeline_mode=pl.Buffered(1))


def _rms_rows(x, g):
    ms = jnp.mean(x * x, axis=-1, keepdims=True)
    return x * lax.rsqrt(ms + EPS) * g


def _head_id(shape, axis):
    return lax.shift_right_logical(lax.broadcasted_iota(jnp.int32, shape, axis), HEAD_DIM_LOG2)


def _head_block_ones():
    shape = (BRANCH_WIDTH, BRANCH_WIDTH)
    return (_head_id(shape, 0) == _head_id(shape, 1)).astype(BF16)


def _head_sumsq(t, ones_bd):
    return jnp.dot((t * t).astype(BF16), ones_bd, preferred_element_type=F32)


def _head_rms(t, g, ones_bd):
    ms = _head_sumsq(t, ones_bd) * (1.0 / HEAD_DIM)
    return t * lax.rsqrt(ms + EPS) * g


def _row_halves(n):
    return [slice(0, n // 2), slice(n // 2, n)]


def _row_pairs(n):
    parts = [slice(i, i + DENSE_PART_ROWS) for i in range(0, n, DENSE_PART_ROWS)]
    return [parts[i:i + 2] for i in range(0, len(parts), 2)]


def _log_sigmoid(x):
    return jnp.minimum(x, 0.0) - jnp.log1p(jnp.exp(-jnp.abs(x)))


def _dot_nt(a, b):
    return lax.dot_general(a, b, (((1,), (1,)), ((), ())), preferred_element_type=F32)


def _dot_tn(a, b):
    return lax.dot_general(a, b, (((0,), (0,)), ((), ())), preferred_element_type=F32)


BF16_SUBLANES = 16


def _cast_slabs(rows, n_steps):
    for steps in (n_steps, n_steps // 2):
        if rows % (steps * BF16_SUBLANES) == 0:
            return rows // steps, steps
    raise ValueError(f"cannot split {rows} weight rows over {n_steps} grid steps")


def _cast_specs(cast_weights, cast_layer, n_steps):
    src_specs, dst_specs, out_shapes = [], [], []
    for w in cast_weights:
        rows, steps = _cast_slabs(w.shape[1], n_steps)
        src_specs.append(pl.BlockSpec((None, rows, w.shape[2]),
                                      lambda i, steps=steps: (cast_layer, jnp.minimum(i, steps - 1), 0)))
        dst_specs.append(pl.BlockSpec((rows, w.shape[2]), lambda i, steps=steps: (jnp.minimum(i, steps - 1), 0)))
        out_shapes.append(jax.ShapeDtypeStruct(w.shape[1:], BF16))
    return src_specs, dst_specs, out_shapes


def _cast_resident_slabs(cast_src, cast_dst):
    for src, dst in zip(cast_src, cast_dst):
        dst[...] = src[...].astype(dst.dtype)


def _swap_rotary_halves(t):
    half = HEAD_DIM // 2
    lane = lax.broadcasted_iota(jnp.int32, (1, LANES), 1)
    in_first_half = jnp.bitwise_and(lane, HEAD_DIM - 1) < half
    blocks = []
    for c0 in range(0, t.shape[1], LANES):
        tb = t[:, c0:c0 + LANES]
        blocks.append(jnp.where(in_first_half, pltpu.roll(tb, LANES - half, 1), pltpu.roll(tb, half, 1)))
    return jnp.concatenate(blocks, axis=1)


def _inproj_kernel(*refs, n_cast):
    x_ref, g_ref, w_ref, cos_ref, sin_ref, gnq_ref, gnk_ref, gmq_ref = refs[:8]
    n_proj = w_ref.shape[1] // BRANCH_WIDTH
    o_refs = refs[8 + n_cast:8 + n_cast + n_proj]
    _cast_resident_slabs(refs[8:8 + n_cast], refs[8 + n_cast + n_proj:])
    bw = BRANCH_WIDTH
    ones_bd = _head_block_ones()
    halves = _row_halves(x_ref.shape[0])
    hs = [_rms_rows(x_ref[r, :], g_ref[...]).astype(BF16) for r in halves]
    ps = [jnp.dot(h, w_ref[...], preferred_element_type=F32) for h in hs]
    for r, p in zip(halves, ps):
        cos, sin = cos_ref[r, :], sin_ref[r, :]

        def store(c0, val):
            for c in range(0, val.shape[1], bw):
                o_ref = o_refs[(c0 + c) // bw]
                o_ref[r, :] = val[:, c:c + bw].astype(o_ref.dtype)

        for c0, scale in ((0, QK_SCALE), (bw, None)):
            t = p[:, c0:c0 + bw]
            rot = t * cos + _swap_rotary_halves(t) * sin
            store(c0, rot if scale is None else rot * scale)
        store(2 * bw, p[:, 2 * bw:5 * bw])
        store(5 * bw, _head_rms(p[:, 5 * bw:6 * bw], gnq_ref[...], ones_bd) * QK_SCALE)
        store(6 * bw, _head_rms(p[:, 6 * bw:7 * bw], gnk_ref[...], ones_bd))
        store(7 * bw, p[:, 7 * bw:8 * bw])
        store(8 * bw, _head_rms(p[:, 8 * bw:9 * bw], gmq_ref[...], ones_bd) * QK_SCALE)


def _inproj(x2, norm_g, w_in, cos, sin, na_gq, na_gk, mem_gq, layer, tm, cast_weights=()):
    t, d = x2.shape
    n_out = w_in.shape[-1]
    bw = BRANCH_WIDTH
    n_proj = n_out // bw
    seq_tiles = cos.shape[0] // tm
    head_gain = _resident((None, 1, bw), lambda i: (layer, 0, 0))
    src_specs, dst_specs, cast_shapes = _cast_specs(cast_weights, layer, t // tm)
    outs = pl.pallas_call(
        functools.partial(_inproj_kernel, n_cast=len(cast_weights)),
        grid=(t // tm,),
        in_specs=[
            pl.BlockSpec((tm, d), lambda i: (i, 0)),
            _resident((None, 1, d), lambda i: (layer, 0, 0)),
            _resident((d, n_out), lambda i: (0, 0)),
            pl.BlockSpec((tm, bw), lambda i: (i % seq_tiles, 0)),
            pl.BlockSpec((tm, bw), lambda i: (i % seq_tiles, 0)),
            head_gain, head_gain, head_gain,
            *src_specs,
        ],
        out_specs=[*[pl.BlockSpec((tm, bw), lambda i: (i, 0))] * n_proj, *dst_specs],
        out_shape=[*[jax.ShapeDtypeStruct((t, bw), BF16)] * n_proj, *cast_shapes],
        compiler_params=_cparams(1),
        name="inproj",
    )(x2, norm_g, w_in, cos, sin, na_gq, na_gk, mem_gq, *cast_weights)
    return list(outs[:n_proj]), list(outs[n_proj:])


def _ret_kernel(q_ref, k_ref, v_ref, g_ref, dec_ref, gn_ref, o_ref, sf_ref, sb_ref, st_ref, dm_ref):
    c = RET_CHUNK
    n_chunks = q_ref.shape[0] // c
    head = _head_id((1, BRANCH_WIDTH), 1)

    lg = _log_sigmoid(dec_ref[...])
    lg2 = jnp.concatenate([lg, lg], axis=1)

    def per_lane(row0):
        out = jnp.zeros((1, BRANCH_WIDTH), F32)
        for h in range(N_HEADS):
            out = jnp.where(head == h, lg2[row0 + h:row0 + h + 1, :], out)
        return out

    lgf, lgb = per_lane(0), per_lane(N_HEADS)

    idx = lax.broadcasted_iota(jnp.int32, (c, 1), 0).astype(F32)
    kdec_f = jnp.exp((c - 1 - idx) * lgf)
    qdec_f = jnp.exp((idx + 1) * lgf)
    kdec_b = jnp.exp(idx * lgb)
    qdec_b = jnp.exp((c - idx) * lgb)
    cdec_f = jnp.exp(c * lgf)
    cdec_b = jnp.exp(c * lgb)

    diff = (lax.broadcasted_iota(jnp.int32, (c, c), 0)
            - lax.broadcasted_iota(jnp.int32, (c, c), 1)).astype(F32)
    for h in range(N_HEADS):
        lf = lg[h:h + 1, :]
        lb = lg[N_HEADS + h:N_HEADS + h + 1, :]
        dm_ref[:, h * c:(h + 1) * c] = jnp.where(diff >= 0, jnp.exp(jnp.maximum(diff, 0.0) * lf),
                                                 jnp.exp(jnp.maximum(-diff, 0.0) * lb))

    state_shape = (BRANCH_WIDTH, BRANCH_WIDTH)
    same_head = _head_id(state_shape, 0) == _head_id(state_shape, 1)

    group = RET_GROUP
    chunk_rows = lambda j: pl.ds(pl.multiple_of(j * c, c), c)

    def sweep(kdec, cdec, states_ref, chunk_of):
        st_ref[...] = jnp.zeros_like(st_ref)

        def body(step, carry):
            chunks = [chunk_of(step * group + g) for g in range(group)]
            kvs = [_dot_tn((k_ref[chunk_rows(j), :].astype(F32) * kdec).astype(BF16), v_ref[chunk_rows(j), :])
                   for j in chunks]
            state = st_ref[...]
            for j, kv in zip(chunks, kvs):
                states_ref[j] = state.astype(BF16)
                state = cdec * state + jnp.where(same_head, kv, 0.0)
            st_ref[...] = state
            return carry

        lax.fori_loop(0, n_chunks // group, body, 0)

    sweep(kdec_f, cdec_f, sf_ref, lambda i: i)
    sweep(kdec_b, cdec_b, sb_ref, lambda i: n_chunks - 1 - i)

    ones_bd = _head_block_ones()
    gn = gn_ref[...]

    def out_body(step, carry):
        chunks = [step * group + g for g in range(group)]
        accs = []
        for j in chunks:
            rows = chunk_rows(j)
            qb = q_ref[rows, :]
            q = qb.astype(F32)
            cross = (jnp.dot((q * qdec_f).astype(BF16), sf_ref[j], preferred_element_type=F32)
                     + jnp.dot((q * qdec_b).astype(BF16), sb_ref[j], preferred_element_type=F32))
            scores = _dot_nt(qb, _stack_heads(k_ref[rows, :], head))
            accs.append((cross, scores))
        accs = [cross + jnp.dot((scores * dm_ref[...]).astype(BF16), _stack_heads(v_ref[chunk_rows(j), :], head),
                                preferred_element_type=F32)
                for j, (cross, scores) in zip(chunks, accs)]
        sumsq = [_head_sumsq(acc, ones_bd) for acc in accs]
        for j, acc, ss in zip(chunks, accs, sumsq):
            rows = chunk_rows(j)
            y = acc * lax.rsqrt(ss * (1.0 / HEAD_DIM) + EPS) * gn
            gate = g_ref[rows, :].astype(F32)
            o_ref[rows, :] = (y * (gate * jax.nn.sigmoid(gate))).astype(o_ref.dtype)
        return carry

    lax.fori_loop(0, n_chunks // group, out_body, 0)


def _per_batch(s):
    return pl.BlockSpec((None, s, BRANCH_WIDTH), lambda i: (i, 0, 0))


def _retention(q, k, v, g, dec, gn, layer):
    b, s, bw = q.shape
    n_chunks = s // RET_CHUNK
    return pl.pallas_call(
        _ret_kernel,
        grid=(b,),
        in_specs=[
            _per_batch(s), _per_batch(s), _per_batch(s), _per_batch(s),
            _resident((None, 2 * N_HEADS, LANES), lambda i: (layer, 0, 0)),
            _resident((None, 1, bw), lambda i: (layer, 0, 0)),
        ],
        out_specs=_per_batch(s),
        out_shape=jax.ShapeDtypeStruct((b, s, bw), BF16),
        scratch_shapes=[
            pltpu.VMEM((n_chunks, bw, bw), BF16),
            pltpu.VMEM((n_chunks, bw, bw), BF16),
            pltpu.VMEM((bw, bw), F32),
            pltpu.VMEM((RET_CHUNK, N_HEADS * RET_CHUNK), F32),
        ],
        compiler_params=_cparams(1),
        name="retention",
    )(q, k, v, g, dec, gn)


def _pool_kernel(v_ref, w_ref, sc_ref, o_ref, pad_ref):
    s = v_ref.shape[0]
    pad = POOL_PAD
    tile = 512
    ext = tile + 2 * pad

    pad_ref[0:pad, :] = jnp.zeros((pad, BRANCH_WIDTH), F32)
    pad_ref[pad + s:pad + s + pad, :] = jnp.zeros((pad, BRANCH_WIDTH), F32)
    pad_ref[pad:pad + s, :] = v_ref[...].astype(F32)

    grp = _head_id((1, BRANCH_WIDTH), 1)
    halfw = jnp.where(grp == 0, POOL_WINDOWS[0] // 2,
                      jnp.where(grp == 1, POOL_WINDOWS[1] // 2,
                                jnp.where(grp == 2, POOL_WINDOWS[2] // 2, POOL_WINDOWS[3] // 2)))

    half_lanes = BRANCH_WIDTH // 2
    grp_lo, grp_hi = grp[:, :half_lanes], grp[:, half_lanes:]
    for ci in range(s // tile):
        p = pad_ref[ci * tile:ci * tile + ext, :]
        s2 = p + pltpu.roll(p, 1, 0)
        s4 = pltpu.roll(s2, 1, 0) + pltpu.roll(s2, ext - 1, 0)
        s4_hi = s4[:, half_lanes:]
        s8 = pltpu.roll(s4_hi, 2, 0) + pltpu.roll(s4_hi, ext - 2, 0)
        s16 = pltpu.roll(s8, 4, 0) + pltpu.roll(s8, ext - 4, 0)
        win = jnp.concatenate([jnp.where(grp_lo == 0, s2[:, :half_lanes], s4[:, :half_lanes]),
                               jnp.where(grp_hi == 2, s8, s16)], axis=1)
        win = win[pad:pad + tile, :]
        if 0 < ci < s // tile - 1:
            pooled = win * (0.5 / halfw.astype(F32)) - p[pad:pad + tile, :]
        else:
            t = ci * tile + lax.broadcasted_iota(jnp.int32, (tile, 1), 0)
            count = (jnp.minimum(t + halfw, s) - jnp.maximum(t - halfw, 0)).astype(F32)
            pooled = win / count - p[pad:pad + tile, :]
        mixed = jnp.dot(pooled.astype(BF16), w_ref[...], preferred_element_type=F32) * sc_ref[...]
        o_ref[ci * tile:(ci + 1) * tile, :] = mixed.astype(o_ref.dtype)


def _pooling(v, w_bd, scale, layer):
    b, s, bw = v.shape
    return pl.pallas_call(
        _pool_kernel,
        grid=(b,),
        in_specs=[
            _per_batch(s),
            _resident((None, bw, bw), lambda i: (layer, 0, 0)),
            _resident((None, 1, bw), lambda i: (layer, 0, 0)),
        ],
        out_specs=_per_batch(s),
        out_shape=jax.ShapeDtypeStruct((b, s, bw), BF16),
        scratch_shapes=[pltpu.VMEM((s + 2 * POOL_PAD, bw), F32)],
        compiler_params=_cparams(1),
        name="pooling",
    )(v, w_bd, scale)


def _stack_heads(t, head_vec):
    return jnp.concatenate([jnp.where(head_vec == h, t, jnp.zeros_like(t)) for h in range(N_HEADS)], axis=0)


def _unstack_heads(t, head_vec, n):
    out = jnp.where(head_vec == 0, t[0:n], 0.0)
    for h in range(1, N_HEADS):
        out = out + jnp.where(head_vec == h, t[h * n:(h + 1) * n], 0.0)
    return out


def _na_pattern_first_offset(pat, rows):
    wr = NA_WIN_ROWS
    query_row = pat if pat <= wr // 2 else rows - wr + pat
    first_key_row = min(max(query_row - wr // 2, 0), rows - wr)
    return first_key_row - query_row + wr - 1


def _na_build_bias(tab_ref, toep_ref, bias_ref, rows):
    wr, wc = NA_WIN_ROWS, NA_WIN_COLS
    n_rel = N_HEADS * (2 * wr - 1)
    x = jnp.concatenate([jnp.broadcast_to(tab_ref[i:i + 1, :], (GRID_W, LANES)) for i in range(n_rel)], axis=0)
    qcol = jnp.bitwise_and(lax.broadcasted_iota(jnp.int32, x.shape, 0), GRID_W - 1)
    for bit in range(GRID_W.bit_length() - 1):
        x = jnp.where(jnp.bitwise_and(qcol, 1 << bit) != 0, pltpu.roll(x, 1 << bit, 1), x)
    toep_ref[...] = x

    lane = lax.broadcasted_iota(jnp.int32, (GRID_W, LANES), 1)
    kcol = jnp.bitwise_and(lane, GRID_W - 1)
    qwin = jnp.clip(lax.broadcasted_iota(jnp.int32, (GRID_W, LANES), 0) - wc // 2, 0, GRID_W - wc)
    in_window = (kcol >= qwin) & (kcol < qwin + wc)
    first_half = lane < GRID_W
    for pat in range(wr):
        dr0 = _na_pattern_first_offset(pat, rows)
        for h in range(N_HEADS):
            for pair in range(wr // 2):
                i0 = h * (2 * wr - 1) + dr0 + 2 * pair
                t = jnp.where(first_half, toep_ref[i0 * GRID_W:(i0 + 1) * GRID_W, :],
                              toep_ref[(i0 + 1) * GRID_W:(i0 + 2) * GRID_W, :])
                bias_ref[pat, h * GRID_W:(h + 1) * GRID_W, pair * LANES:(pair + 1) * LANES] = (
                    jnp.where(in_window, t, NEG_INF))


def _na_kernel(q_ref, k_ref, v_ref, tab_ref, o_ref, toep_ref, bias_ref):
    s = q_ref.shape[0]
    rows = s // GRID_W
    wr = NA_WIN_ROWS
    head_v = _head_id((1, BRANCH_WIDTH), 1)

    @pl.when(pl.program_id(0) == 0)
    def _():
        _na_build_bias(tab_ref, toep_ref, bias_ref, rows)

    def group_body(step, carry):
        slices, scores = [], []
        for g in range(NA_GROUP):
            r = step * NA_GROUP + g
            r0 = jnp.clip(r - wr // 2, 0, rows - wr)
            pat = jnp.where(r < wr // 2, r, jnp.where(r > rows - wr // 2, r - (rows - wr), wr // 2))
            qrows = pl.ds(pl.multiple_of(r * GRID_W, GRID_W), GRID_W)
            krows = pl.ds(pl.multiple_of(r0 * GRID_W, GRID_W), wr * GRID_W)
            qst = _stack_heads(q_ref[qrows, :], head_v)
            scores.append(_dot_nt(qst, k_ref[krows, :]) + bias_ref[pat])
            slices.append((qrows, krows))
        probs = []
        for sc in scores:
            e = jnp.exp(sc - jnp.max(sc, axis=-1, keepdims=True))
            probs.append((e.astype(BF16), 1.0 / jnp.sum(e, axis=-1, keepdims=True)))
        pvs = [jnp.dot(e, v_ref[krows, :], preferred_element_type=F32) * inv_l
               for (e, inv_l), (_, krows) in zip(probs, slices)]
        for pv, (qrows, _) in zip(pvs, slices):
            o_ref[qrows, :] = _unstack_heads(pv, head_v, GRID_W).astype(o_ref.dtype)
        return carry

    lax.fori_loop(0, rows // NA_GROUP, group_body, 0)


def _neighbourhood(q, k, v, rpb_tab, layer):
    b, s, bw = q.shape
    n_tab = rpb_tab.shape[1]
    return pl.pallas_call(
        _na_kernel,
        grid=(b,),
        in_specs=[
            _per_batch(s), _per_batch(s), _per_batch(s),
            _resident((None, n_tab, LANES), lambda i: (layer, 0, 0)),
        ],
        out_specs=_per_batch(s),
        out_shape=jax.ShapeDtypeStruct((b, s, bw), BF16),
        scratch_shapes=[
            pltpu.VMEM((N_HEADS * (2 * NA_WIN_ROWS - 1) * GRID_W, LANES), F32),
            pltpu.VMEM((NA_WIN_ROWS, N_HEADS * GRID_W, NA_WIN_ROWS * GRID_W), F32),
        ],
        compiler_params=_cparams(1),
        name="neighbourhood",
    )(q, k, v, rpb_tab)


def _na_rpb_table(rpb):
    n_layers, n_heads, n_dr, n_dc = rpb.shape
    t = jnp.pad(rpb.astype(F32), ((0, 0), (0, 0), (0, 0), (0, GRID_W - n_dc)))
    t = jnp.roll(t, -(NA_WIN_COLS - 1), axis=-1)
    t = jnp.tile(t, (1, 1, 1, LANES // GRID_W)).reshape(n_layers, n_heads * n_dr, LANES)
    return jnp.pad(t, ((0, 0), (0, -(n_heads * n_dr) % 8), (0, 0)))


def _memkv_kernel(m_ref, g_ref, w_ref, gk_ref, k_ref, v_ref):
    n_batch = k_ref.shape[0]
    n_mem = k_ref.shape[1] // N_HEADS
    bw = BRANCH_WIDTH
    head_v = _head_id((1, bw), 1)
    ones_bd = _head_block_ones()
    parts = [slice(i * n_mem, (i + 1) * n_mem) for i in range(n_batch)]
    mns = [_rms_rows(m_ref[r, :], g_ref[...]).astype(BF16) for r in parts]
    kvs = [jnp.dot(mn, w_ref[...], preferred_element_type=F32) for mn in mns]
    for i, kv in enumerate(kvs):
        kn = _head_rms(kv[:, :bw], gk_ref[...], ones_bd)
        k_ref[i] = _stack_heads(kn.astype(k_ref.dtype), head_v)
        v_ref[i] = _stack_heads(kv[:, bw:].astype(v_ref.dtype), head_v)


def _memkv(mem, norm_g, w_kv, gk):
    b, m, d = mem.shape
    n_layers = w_kv.shape[0]
    bw = BRANCH_WIDTH
    out = jax.ShapeDtypeStruct((n_layers, b, N_HEADS * m, bw), BF16)
    return pl.pallas_call(
        _memkv_kernel,
        grid=(n_layers,),
        in_specs=[
            _resident((b * m, d), lambda l: (0, 0)),
            pl.BlockSpec((None, 1, d), lambda l: (l, 0, 0)),
            pl.BlockSpec((None, d, 2 * bw), lambda l: (l, 0, 0)),
            pl.BlockSpec((None, 1, bw), lambda l: (l, 0, 0)),
        ],
        out_specs=[pl.BlockSpec((None, b, N_HEADS * m, bw), lambda l: (l, 0, 0, 0))] * 2,
        out_shape=[out, out],
        compiler_params=_cparams(1),
        name="memkv",
    )(mem.reshape(b * m, d), norm_g, w_kv, gk)


def _memattn_kernel(q_ref, k_ref, v_ref, o_ref):
    n_mem = k_ref.shape[0] // N_HEADS
    sub = MEM_SUB_ROWS
    head_v = _head_id((1, BRANCH_WIDTH), 1)

    def group_body(step, carry):
        tiles = [pl.ds(pl.multiple_of((step * MEM_GROUP + g) * sub, sub), sub) for g in range(MEM_GROUP)]
        scores = [_dot_nt(q_ref[t, :], k_ref[...]) for t in tiles]
        probs, inv_ls = [], []
        for sc in scores:
            es = []
            inv_l = jnp.zeros((sub, BRANCH_WIDTH), F32)
            for h in range(N_HEADS):
                seg = sc[:, h * n_mem:(h + 1) * n_mem]
                e = jnp.exp(seg - jnp.max(seg, axis=-1, keepdims=True))
                inv_l = jnp.where(head_v == h, 1.0 / jnp.sum(e, axis=-1, keepdims=True), inv_l)
                es.append(e.astype(BF16))
            probs.append(jnp.concatenate(es, axis=1))
            inv_ls.append(inv_l)
        pvs = [jnp.dot(p, v_ref[...], preferred_element_type=F32) for p in probs]
        for t, pv, inv_l in zip(tiles, pvs, inv_ls):
            o_ref[t, :] = (pv * inv_l).astype(o_ref.dtype)
        return carry

    lax.fori_loop(0, q_ref.shape[0] // (sub * MEM_GROUP), group_body, 0)


def _memattn(q, mk, mv, layer):
    b, s, bw = q.shape
    m = mk.shape[2]
    return pl.pallas_call(
        _memattn_kernel,
        grid=(b,),
        in_specs=[
            _per_batch(s),
            pl.BlockSpec((None, None, m, bw), lambda i: (layer, i, 0, 0)),
            pl.BlockSpec((None, None, m, bw), lambda i: (layer, i, 0, 0)),
        ],
        out_specs=_per_batch(s),
        out_shape=jax.ShapeDtypeStruct((b, s, bw), BF16),
        compiler_params=_cparams(1),
        name="memattn",
    )(q, mk, mv)


def _merge_kernel(x_ref, g_ref, b0_ref, b1_ref, b2_ref, b3_ref, wg_ref, wb_ref, wo_ref, o_ref):
    d = x_ref.shape[-1]
    for halves in _row_pairs(x_ref.shape[0]):
        hs = [_rms_rows(x_ref[r, :], g_ref[...]).astype(BF16) for r in halves]
        merged = [None] * len(halves)
        for n, br_ref in enumerate((b0_ref, b1_ref, b2_ref, b3_ref)):
            for i, r in enumerate(halves):
                gate = jax.nn.sigmoid(jnp.dot(hs[i], wg_ref[:, n * d:(n + 1) * d], preferred_element_type=F32))
                up = jnp.dot(br_ref[r, :], wb_ref[n * BRANCH_WIDTH:(n + 1) * BRANCH_WIDTH, :],
                             preferred_element_type=F32)
                merged[i] = gate * up if merged[i] is None else merged[i] + gate * up
        outs = [jnp.dot(m.astype(BF16), wo_ref[...], preferred_element_type=F32) for m in merged]
        for r, t in zip(halves, outs):
            o_ref[r, :] = x_ref[r, :] + t


def _merge(x2, norm_g, branches, w_gate, w_branch, w_out, layer, tm):
    t, d = x2.shape
    bw = BRANCH_WIDTH
    br_spec = pl.BlockSpec((tm, bw), lambda i: (i, 0))
    return pl.pallas_call(
        _merge_kernel,
        grid=(t // tm,),
        in_specs=[
            pl.BlockSpec((tm, d), lambda i: (i, 0)),
            _resident((None, 1, d), lambda i: (layer, 0, 0)),
            br_spec, br_spec, br_spec, br_spec,
            _resident((d, N_BRANCH * d), lambda i: (0, 0)),
            _resident((N_BRANCH * bw, d), lambda i: (0, 0)),
            _resident((d, d), lambda i: (0, 0)),
        ],
        out_specs=pl.BlockSpec((tm, d), lambda i: (i, 0)),
        out_shape=jax.ShapeDtypeStruct((t, d), F32),
        compiler_params=_cparams(1),
        name="merge",
    )(x2, norm_g, *branches, w_gate, w_branch, w_out)


def _ffn_kernel(*refs, n_cast):
    x_ref, g_ref, w1_ref, w2_ref = refs[:4]
    o_ref = refs[4 + n_cast]
    _cast_resident_slabs(refs[4:4 + n_cast], refs[5 + n_cast:])
    ff = w2_ref.shape[0]
    halves = _row_halves(x_ref.shape[0])
    hn = [_rms_rows(x_ref[r, :], g_ref[...]).astype(BF16) for r in halves]
    ag = [jnp.dot(h, w1_ref[...], preferred_element_type=F32) for h in hn]
    u = [(t[:, :ff] * jax.nn.sigmoid(t[:, :ff]) * t[:, ff:]).astype(BF16) for t in ag]
    down = [jnp.dot(t, w2_ref[...], preferred_element_type=F32) for t in u]
    for r, t in zip(halves, down):
        o_ref[r, :] = x_ref[r, :] + t


def _ffn(x2, norm_g, w1, w2, layer, tm, cast_weights=()):
    t, d = x2.shape
    ff = w2.shape[0]
    src_specs, dst_specs, cast_shapes = _cast_specs(cast_weights, layer + 1, t // tm)
    outs = pl.pallas_call(
        functools.partial(_ffn_kernel, n_cast=len(cast_weights)),
        grid=(t // tm,),
        in_specs=[
            pl.BlockSpec((tm, d), lambda i: (i, 0)),
            _resident((None, 1, d), lambda i: (layer, 0, 0)),
            _resident((d, 2 * ff), lambda i: (0, 0)),
            _resident((ff, d), lambda i: (0, 0)),
            *src_specs,
        ],
        out_specs=[pl.BlockSpec((tm, d), lambda i: (i, 0)), *dst_specs],
        out_shape=[jax.ShapeDtypeStruct((t, d), F32), *cast_shapes],
        compiler_params=_cparams(1),
        name="ffn",
    )(x2, norm_g, w1, w2, *cast_weights)
    return outs[0], list(outs[1:])


def _block_diag(w):
    n_layers, g, c, e = w.shape
    eye = jnp.eye(g, dtype=w.dtype)
    return jnp.einsum('lgce,gk->lgcke', w, eye).reshape(n_layers, g * c, g * e)


def kernel(x, mem, norm_mix_g, norm_mem_g, w_in, w_gate, ret_decay_fwd, ret_decay_bwd, ret_norm_g, pool_w,
           pool_scale, na_q_norm_g, na_k_norm_g, na_rpb, mem_q_norm_g, mem_k_norm_g, w_mem_kv, w_branch, w_out,
           norm_ffn_g, w_ffn_in, w_ffn_out):
    b, s, d = x.shape
    n_layers = w_in.shape[0]
    bw = BRANCH_WIDTH

    dense_f32 = [w_in, w_gate, w_branch.reshape(n_layers, N_BRANCH * bw, d), w_out, w_ffn_in, w_ffn_out]

    w_in_b = w_in[0].astype(BF16)
    dense_rest_b = None
    w_mem_kv_b = w_mem_kv.astype(BF16)
    pool_w_bd = _block_diag(pool_w).astype(BF16)

    row3 = lambda a: a.astype(F32).reshape(n_layers, 1, -1)
    tile_heads = lambda g: jnp.tile(g.astype(F32), (1, N_HEADS)).reshape(n_layers, 1, bw)
    norm_mix_g3, norm_mem_g3, norm_ffn_g3 = row3(norm_mix_g), row3(norm_mem_g), row3(norm_ffn_g)
    ret_norm_g3, pool_scale3 = row3(ret_norm_g), row3(pool_scale)
    na_gq, na_gk = tile_heads(na_q_norm_g), tile_heads(na_k_norm_g)
    mem_gq, mem_gk = tile_heads(mem_q_norm_g), tile_heads(mem_k_norm_g)
    dec = jnp.concatenate([ret_decay_fwd, ret_decay_bwd], axis=1).astype(F32)
    dec = jnp.broadcast_to(dec[:, :, None], (n_layers, 2 * N_HEADS, LANES))
    rpb_tab = _na_rpb_table(na_rpb)

    half = HEAD_DIM // 2
    inv = ROPE_THETA ** (-jnp.arange(half, dtype=F32) / half)
    ang = jnp.arange(s, dtype=F32)[:, None] * inv[None, :]
    cos = jnp.tile(jnp.concatenate([jnp.cos(ang), jnp.cos(ang)], axis=1), (1, N_HEADS))
    sin = jnp.tile(jnp.concatenate([-jnp.sin(ang), jnp.sin(ang)], axis=1), (1, N_HEADS))

    mk_all, mv_all = _memkv(mem, norm_mem_g3, w_mem_kv_b, mem_gk)

    x2 = x.reshape(b * s, d)
    for layer in range(n_layers):
        proj, cast_now = _inproj(x2, norm_mix_g3, w_in_b, cos, sin, na_gq, na_gk, mem_gq, layer,
                                 tm=INPROJ_TILE_ROWS, cast_weights=dense_f32[1:] if layer == 0 else [])
        rq, rk, rv, rg, pv, nq, nk, nv, mq = [t.reshape(b, s, bw) for t in proj]
        w_gate_b, w_branch_b, w_out_b, w_ffn_in_b, w_ffn_out_b = cast_now if layer == 0 else dense_rest_b
        ret = _retention(rq, rk, rv, rg, dec, ret_norm_g3, layer)
        pool = _pooling(pv, pool_w_bd, pool_scale3, layer)
        na = _neighbourhood(nq, nk, nv, rpb_tab, layer)
        mo = _memattn(mq, mk_all, mv_all, layer)
        branches = [t.reshape(b * s, bw) for t in (ret, pool, na, mo)]
        x2 = _merge(x2, norm_mix_g3, branches, w_gate_b, w_branch_b, w_out_b, layer, tm=MERGE_TILE_ROWS)
        later = dense_f32 if layer + 1 < n_layers else []
        x2, cast_next = _ffn(x2, norm_ffn_g3, w_ffn_in_b, w_ffn_out_b, layer, tm=FFN_TILE_ROWS, cast_weights=later)
        if cast_next:
            w_in_b, dense_rest_b = cast_next[0], cast_next[1:]
    return x2.reshape(b, s, d)
```

```python
import functools

import jax
import jax.numpy as jnp
from jax import lax
from jax.experimental import pallas as pl
from jax.experimental.pallas import tpu as pltpu

F32 = jnp.float32
BF16 = jnp.bfloat16

HEAD_DIM = 64
HEAD_DIM_LOG2 = HEAD_DIM.bit_length() - 1
BRANCH_WIDTH = 256
N_HEADS = BRANCH_WIDTH // HEAD_DIM
N_BRANCH = 4
GRID_W = 64
RET_CHUNK = 128
RET_GROUP = 16
NA_GROUP = 32
MEM_SUB_ROWS = 128
MEM_GROUP = 16
ROPE_THETA = 10000.0
POOL_WINDOWS = (2, 4, 8, 16)
POOL_PAD = 16
NA_WIN_ROWS = 8
NA_WIN_COLS = 16
NEG_INF = -1e30
EPS = 1e-6
QK_SCALE = HEAD_DIM ** -0.5

LANES = 128
VMEM_LIMIT_BYTES = 56 * 1024 * 1024

INPROJ_TILE_ROWS = 1024
MERGE_TILE_ROWS = 1024
DENSE_PART_ROWS = 256
FFN_TILE_ROWS = 1024


def _cparams(n_grid_dims):
    return pltpu.CompilerParams(
        dimension_semantics=("arbitrary",) * n_grid_dims,
        vmem_limit_bytes=VMEM_LIMIT_BYTES,
    )


def _resident(block_shape, index_map):
    return pl.BlockSpec(block_shape, index_map, pipeline_mode=pl.Buffered(1))


def _rms_rows(x, g):
    ms = jnp.mean(x * x, axis=-1, keepdims=True)
    return x * lax.rsqrt(ms + EPS) * g


def _head_id(shape, axis):
    return lax.shift_right_logical(lax.broadcasted_iota(jnp.int32, shape, axis), HEAD_DIM_LOG2)


def _head_block_ones():
    shape = (BRANCH_WIDTH, BRANCH_WIDTH)
    return (_head_id(shape, 0) == _head_id(shape, 1)).astype(BF16)


def _head_sumsq(t, ones_bd):
    return jnp.dot((t * t).astype(BF16), ones_bd, preferred_element_type=F32)


def _head_rms(t, g, ones_bd):
    ms = _head_sumsq(t, ones_bd) * (1.0 / HEAD_DIM)
    return t * lax.rsqrt(ms + EPS) * g


def _row_halves(n):
    return [slice(0, n // 2), slice(n // 2, n)]


def _row_pairs(n):
    parts = [slice(i, i + DENSE_PART_ROWS) for i in range(0, n, DENSE_PART_ROWS)]
    return [parts[i:i + 2] for i in range(0, len(parts), 2)]


def _log_sigmoid(x):
    return jnp.minimum(x, 0.0) - jnp.log1p(jnp.exp(-jnp.abs(x)))


def _dot_nt(a, b):
    return lax.dot_general(a, b, (((1,), (1,)), ((), ())), preferred_element_type=F32)


def _dot_tn(a, b):
    return lax.dot_general(a, b, (((0,), (0,)), ((), ())), preferred_element_type=F32)


BF16_SUBLANES = 16


def _cast_slabs(rows, n_steps):
    for steps in (n_steps, n_steps // 2):
        if rows % (steps * BF16_SUBLANES) == 0:
            return rows // steps, steps
    raise ValueError(f"cannot split {rows} weight rows over {n_steps} grid steps")


def _cast_specs(cast_weights, cast_layer, n_steps):
    src_specs, dst_specs, out_shapes = [], [], []
    for w in cast_weights:
        rows, steps = _cast_slabs(w.shape[1], n_steps)
        src_specs.append(pl.BlockSpec((None, rows, w.shape[2]),
                                      lambda i, steps=steps: (cast_layer, jnp.minimum(i, steps - 1), 0)))
        dst_specs.append(pl.BlockSpec((rows, w.shape[2]), lambda i, steps=steps: (jnp.minimum(i, steps - 1), 0)))
        out_shapes.append(jax.ShapeDtypeStruct(w.shape[1:], BF16))
    return src_specs, dst_specs, out_shapes


def _cast_resident_slabs(cast_src, cast_dst):
    for src, dst in zip(cast_src, cast_dst):
        dst[...] = src[...].astype(dst.dtype)


def _swap_rotary_halves(t):
    half = HEAD_DIM // 2
    lane = lax.broadcasted_iota(jnp.int32, (1, LANES), 1)
    in_first_half = jnp.bitwise_and(lane, HEAD_DIM - 1) < half
    blocks = []
    for c0 in range(0, t.shape[1], LANES):
        tb = t[:, c0:c0 + LANES]
        blocks.append(jnp.where(in_first_half, pltpu.roll(tb, LANES - half, 1), pltpu.roll(tb, half, 1)))
    return jnp.concatenate(blocks, axis=1)


def _inproj_kernel(*refs, n_cast):
    x_ref, g_ref, w_ref, cos_ref, sin_ref, gnq_ref, gnk_ref, gmq_ref = refs[:8]
    n_proj = w_ref.shape[1] // BRANCH_WIDTH
    o_refs = refs[8 + n_cast:8 + n_cast + n_proj]
    _cast_resident_slabs(refs[8:8 + n_cast], refs[8 + n_cast + n_proj:])
    bw = BRANCH_WIDTH
    ones_bd = _head_block_ones()
    halves = _row_halves(x_ref.shape[0])
    hs = [_rms_rows(x_ref[r, :], g_ref[...]).astype(BF16) for r in halves]
    ps = [jnp.dot(h, w_ref[...], preferred_element_type=F32) for h in hs]
    for r, p in zip(halves, ps):
        cos, sin = cos_ref[r, :], sin_ref[r, :]

        def store(c0, val):
            for c in range(0, val.shape[1], bw):
                o_ref = o_refs[(c0 + c) // bw]
                o_ref[r, :] = val[:, c:c + bw].astype(o_ref.dtype)

        for c0, scale in ((0, QK_SCALE), (bw, None)):
            t = p[:, c0:c0 + bw]
            rot = t * cos + _swap_rotary_halves(t) * sin
            store(c0, rot if scale is None else rot * scale)
        store(2 * bw, p[:, 2 * bw:5 * bw])
        store(5 * bw, _head_rms(p[:, 5 * bw:6 * bw], gnq_ref[...], ones_bd) * QK_SCALE)
        store(6 * bw, _head_rms(p[:, 6 * bw:7 * bw], gnk_ref[...], ones_bd))
        store(7 * bw, p[:, 7 * bw:8 * bw])
        store(8 * bw, _head_rms(p[:, 8 * bw:9 * bw], gmq_ref[...], ones_bd) * QK_SCALE)


def _inproj(x2, norm_g, w_in, cos, sin, na_gq, na_gk, mem_gq, layer, tm, cast_weights=()):
    t, d = x2.shape
    n_out = w_in.shape[-1]
    bw = BRANCH_WIDTH
    n_proj = n_out // bw
    seq_tiles = cos.shape[0] // tm
    head_gain = _resident((None, 1, bw), lambda i: (layer, 0, 0))
    src_specs, dst_specs, cast_shapes = _cast_specs(cast_weights, layer, t // tm)
    outs = pl.pallas_call(
        functools.partial(_inproj_kernel, n_cast=len(cast_weights)),
        grid=(t // tm,),
        in_specs=[
            pl.BlockSpec((tm, d), lambda i: (i, 0)),
            _resident((None, 1, d), lambda i: (layer, 0, 0)),
            _resident((d, n_out), lambda i: (0, 0)),
            pl.BlockSpec((tm, bw), lambda i: (i % seq_tiles, 0)),
            pl.BlockSpec((tm, bw), lambda i: (i % seq_tiles, 0)),
            head_gain, head_gain, head_gain,
            *src_specs,
        ],
        out_specs=[*[pl.BlockSpec((tm, bw), lambda i: (i, 0))] * n_proj, *dst_specs],
        out_shape=[*[jax.ShapeDtypeStruct((t, bw), BF16)] * n_proj, *cast_shapes],
        compiler_params=_cparams(1),
        name="inproj",
    )(x2, norm_g, w_in, cos, sin, na_gq, na_gk, mem_gq, *cast_weights)
    return list(outs[:n_proj]), list(outs[n_proj:])


def _ret_kernel(q_ref, k_ref, v_ref, g_ref, dec_ref, gn_ref, o_ref, sf_ref, sb_ref, st_ref, dm_ref):
    c = RET_CHUNK
    n_chunks = q_ref.shape[0] // c
    head = _head_id((1, BRANCH_WIDTH), 1)

    lg = _log_sigmoid(dec_ref[...])
    lg2 = jnp.concatenate([lg, lg], axis=1)

    def per_lane(row0):
        out = jnp.zeros((1, BRANCH_WIDTH), F32)
        for h in range(N_HEADS):
            out = jnp.where(head == h, lg2[row0 + h:row0 + h + 1, :], out)
        return out

    lgf, lgb = per_lane(0), per_lane(N_HEADS)

    idx = lax.broadcasted_iota(jnp.int32, (c, 1), 0).astype(F32)
    kdec_f = jnp.exp((c - 1 - idx) * lgf)
    qdec_f = jnp.exp((idx + 1) * lgf)
    kdec_b = jnp.exp(idx * lgb)
    qdec_b = jnp.exp((c - idx) * lgb)
    cdec_f = jnp.exp(c * lgf)
    cdec_b = jnp.exp(c * lgb)

    diff = (lax.broadcasted_iota(jnp.int32, (c, c), 0)
            - lax.broadcasted_iota(jnp.int32, (c, c), 1)).astype(F32)
    for h in range(N_HEADS):
        lf = lg[h:h + 1, :]
        lb = lg[N_HEADS + h:N_HEADS + h + 1, :]
        dm_ref[:, h * c:(h + 1) * c] = jnp.where(diff >= 0, jnp.exp(jnp.maximum(diff, 0.0) * lf),
                                                 jnp.exp(jnp.maximum(-diff, 0.0) * lb))

    state_shape = (BRANCH_WIDTH, BRANCH_WIDTH)
    same_head = _head_id(state_shape, 0) == _head_id(state_shape, 1)

    group = RET_GROUP
    chunk_rows = lambda j: pl.ds(pl.multiple_of(j * c, c), c)

    def sweep(kdec, cdec, states_ref, chunk_of):
        st_ref[...] = jnp.zeros_like(st_ref)

        def body(step, carry):
            chunks = [chunk_of(step * group + g) for g in range(group)]
            kvs = [_dot_tn((k_ref[chunk_rows(j), :].astype(F32) * kdec).astype(BF16), v_ref[chunk_rows(j), :])
                   for j in chunks]
            state = st_ref[...]
            for j, kv in zip(chunks, kvs):
                states_ref[j] = state.astype(BF16)
                state = cdec * state + jnp.where(same_head, kv, 0.0)
            st_ref[...] = state
            return carry

        lax.fori_loop(0, n_chunks // group, body, 0)

    sweep(kdec_f, cdec_f, sf_ref, lambda i: i)
    sweep(kdec_b, cdec_b, sb_ref, lambda i: n_chunks - 1 - i)

    ones_bd = _head_block_ones()
    gn = gn_ref[...]

    def out_body(step, carry):
        chunks = [step * group + g for g in range(group)]
        accs = []
        for j in chunks:
            rows = chunk_rows(j)
            qb = q_ref[rows, :]
            q = qb.astype(F32)
            cross = (jnp.dot((q * qdec_f).astype(BF16), sf_ref[j], preferred_element_type=F32)
                     + jnp.dot((q * qdec_b).astype(BF16), sb_ref[j], preferred_element_type=F32))
            scores = _dot_nt(qb, _stack_heads(k_ref[rows, :], head))
            accs.append((cross, scores))
        accs = [cross + jnp.dot((scores * dm_ref[...]).astype(BF16), _stack_heads(v_ref[chunk_rows(j), :], head),
                                preferred_element_type=F32)
                for j, (cross, scores) in zip(chunks, accs)]
        sumsq = [_head_sumsq(acc, ones_bd) for acc in accs]
        for j, acc, ss in zip(chunks, accs, sumsq):
            rows = chunk_rows(j)
            y = acc * lax.rsqrt(ss * (1.0 / HEAD_DIM) + EPS) * gn
            gate = g_ref[rows, :].astype(F32)
            o_ref[rows, :] = (y * (gate * jax.nn.sigmoid(gate))).astype(o_ref.dtype)
        return carry

    lax.fori_loop(0, n_chunks // group, out_body, 0)


def _per_batch(s):
    return pl.BlockSpec((None, s, BRANCH_WIDTH), lambda i: (i, 0, 0))


def _retention(q, k, v, g, dec, gn, layer):
    b, s, bw = q.shape
    n_chunks = s // RET_CHUNK
    return pl.pallas_call(
        _ret_kernel,
        grid=(b,),
        in_specs=[
            _per_batch(s), _per_batch(s), _per_batch(s), _per_batch(s),
            _resident((None, 2 * N_HEADS, LANES), lambda i: (layer, 0, 0)),
            _resident((None, 1, bw), lambda i: (layer, 0, 0)),
        ],
        out_specs=_per_batch(s),
        out_shape=jax.ShapeDtypeStruct((b, s, bw), BF16),
        scratch_shapes=[
            pltpu.VMEM((n_chunks, bw, bw), BF16),
            pltpu.VMEM((n_chunks, bw, bw), BF16),
            pltpu.VMEM((bw, bw), F32),
            pltpu.VMEM((RET_CHUNK, N_HEADS * RET_CHUNK), F32),
        ],
        compiler_params=_cparams(1),
        name="retention",
    )(q, k, v, g, dec, gn)


def _pool_kernel(v_ref, w_ref, sc_ref, o_ref, pad_ref):
    s = v_ref.shape[0]
    pad = POOL_PAD
    tile = 512
    ext = tile + 2 * pad

    pad_ref[0:pad, :] = jnp.zeros((pad, BRANCH_WIDTH), F32)
    pad_ref[pad + s:pad + s + pad, :] = jnp.zeros((pad, BRANCH_WIDTH), F32)
    pad_ref[pad:pad + s, :] = v_ref[...].astype(F32)

    grp = _head_id((1, BRANCH_WIDTH), 1)
    halfw = jnp.where(grp == 0, POOL_WINDOWS[0] // 2,
                      jnp.where(grp == 1, POOL_WINDOWS[1] // 2,
                                jnp.where(grp == 2, POOL_WINDOWS[2] // 2, POOL_WINDOWS[3] // 2)))

    half_lanes = BRANCH_WIDTH // 2
    grp_lo, grp_hi = grp[:, :half_lanes], grp[:, half_lanes:]
    for ci in range(s // tile):
        p = pad_ref[ci * tile:ci * tile + ext, :]
        s2 = p + pltpu.roll(p, 1, 0)
        s4 = pltpu.roll(s2, 1, 0) + pltpu.roll(s2, ext - 1, 0)
        s4_hi = s4[:, half_lanes:]
        s8 = pltpu.roll(s4_hi, 2, 0) + pltpu.roll(s4_hi, ext - 2, 0)
        s16 = pltpu.roll(s8, 4, 0) + pltpu.roll(s8, ext - 4, 0)
        win = jnp.concatenate([jnp.where(grp_lo == 0, s2[:, :half_lanes], s4[:, :half_lanes]),
                               jnp.where(grp_hi == 2, s8, s16)], axis=1)
        win = win[pad:pad + tile, :]
        if 0 < ci < s // tile - 1:
            pooled = win * (0.5 / halfw.astype(F32)) - p[pad:pad + tile, :]
        else:
            t = ci * tile + lax.broadcasted_iota(jnp.int32, (tile, 1), 0)
            count = (jnp.minimum(t + halfw, s) - jnp.maximum(t - halfw, 0)).astype(F32)
            pooled = win / count - p[pad:pad + tile, :]
        mixed = jnp.dot(pooled.astype(BF16), w_ref[...], preferred_element_type=F32) * sc_ref[...]
        o_ref[ci * tile:(ci + 1) * tile, :] = mixed.astype(o_ref.dtype)


def _pooling(v, w_bd, scale, layer):
    b, s, bw = v.shape
    return pl.pallas_call(
        _pool_kernel,
        grid=(b,),
        in_specs=[
            _per_batch(s),
            _resident((None, bw, bw), lambda i: (layer, 0, 0)),
            _resident((None, 1, bw), lambda i: (layer, 0, 0)),
        ],
        out_specs=_per_batch(s),
        out_shape=jax.ShapeDtypeStruct((b, s, bw), BF16),
        scratch_shapes=[pltpu.VMEM((s + 2 * POOL_PAD, bw), F32)],
        compiler_params=_cparams(1),
        name="pooling",
    )(v, w_bd, scale)


def _stack_heads(t, head_vec):
    return jnp.concatenate([jnp.where(head_vec == h, t, jnp.zeros_like(t)) for h in range(N_HEADS)], axis=0)


def _unstack_heads(t, head_vec, n):
    out = jnp.where(head_vec == 0, t[0:n], 0.0)
    for h in range(1, N_HEADS):
        out = out + jnp.where(head_vec == h, t[h * n:(h + 1) * n], 0.0)
    return out


def _na_pattern_first_offset(pat, rows):
    wr = NA_WIN_ROWS
    query_row = pat if pat <= wr // 2 else rows - wr + pat
    first_key_row = min(max(query_row - wr // 2, 0), rows - wr)
    return first_key_row - query_row + wr - 1


def _na_build_bias(tab_ref, toep_ref, bias_ref, rows):
    wr, wc = NA_WIN_ROWS, NA_WIN_COLS
    n_rel = N_HEADS * (2 * wr - 1)
    x = jnp.concatenate([jnp.broadcast_to(tab_ref[i:i + 1, :], (GRID_W, LANES)) for i in range(n_rel)], axis=0)
    qcol = jnp.bitwise_and(lax.broadcasted_iota(jnp.int32, x.shape, 0), GRID_W - 1)
    for bit in range(GRID_W.bit_length() - 1):
        x = jnp.where(jnp.bitwise_and(qcol, 1 << bit) != 0, pltpu.roll(x, 1 << bit, 1), x)
    toep_ref[...] = x

    lane = lax.broadcasted_iota(jnp.int32, (GRID_W, LANES), 1)
    kcol = jnp.bitwise_and(lane, GRID_W - 1)
    qwin = jnp.clip(lax.broadcasted_iota(jnp.int32, (GRID_W, LANES), 0) - wc // 2, 0, GRID_W - wc)
    in_window = (kcol >= qwin) & (kcol < qwin + wc)
    first_half = lane < GRID_W
    for pat in range(wr):
        dr0 = _na_pattern_first_offset(pat, rows)
        for h in range(N_HEADS):
            for pair in range(wr // 2):
                i0 = h * (2 * wr - 1) + dr0 + 2 * pair
                t = jnp.where(first_half, toep_ref[i0 * GRID_W:(i0 + 1) * GRID_W, :],
                              toep_ref[(i0 + 1) * GRID_W:(i0 + 2) * GRID_W, :])
                bias_ref[pat, h * GRID_W:(h + 1) * GRID_W, pair * LANES:(pair + 1) * LANES] = (
                    jnp.where(in_window, t, NEG_INF))


def _na_kernel(q_ref, k_ref, v_ref, tab_ref, o_ref, toep_ref, bias_ref):
    s = q_ref.shape[0]
    rows = s // GRID_W
    wr = NA_WIN_ROWS
    head_v = _head_id((1, BRANCH_WIDTH), 1)

    @pl.when(pl.program_id(0) == 0)
    def _():
        _na_build_bias(tab_ref, toep_ref, bias_ref, rows)

    def group_body(step, carry):
        slices, scores = [], []
        for g in range(NA_GROUP):
            r = step * NA_GROUP + g
            r0 = jnp.clip(r - wr // 2, 0, rows - wr)
            pat = jnp.where(r < wr // 2, r, jnp.where(r > rows - wr // 2, r - (rows - wr), wr // 2))
            qrows = pl.ds(pl.multiple_of(r * GRID_W, GRID_W), GRID_W)
            krows = pl.ds(pl.multiple_of(r0 * GRID_W, GRID_W), wr * GRID_W)
            qst = _stack_heads(q_ref[qrows, :], head_v)
            scores.append(_dot_nt(qst, k_ref[krows, :]) + bias_ref[pat])
            slices.append((qrows, krows))
        probs = []
        for sc in scores:
            e = jnp.exp(sc - jnp.max(sc, axis=-1, keepdims=True))
            probs.append((e.astype(BF16), 1.0 / jnp.sum(e, axis=-1, keepdims=True)))
        pvs = [jnp.dot(e, v_ref[krows, :], preferred_element_type=F32) * inv_l
               for (e, inv_l), (_, krows) in zip(probs, slices)]
        for pv, (qrows, _) in zip(pvs, slices):
            o_ref[qrows, :] = _unstack_heads(pv, head_v, GRID_W).astype(o_ref.dtype)
        return carry

    lax.fori_loop(0, rows // NA_GROUP, group_body, 0)


def _neighbourhood(q, k, v, rpb_tab, layer):
    b, s, bw = q.shape
    n_tab = rpb_tab.shape[1]
    return pl.pallas_call(
        _na_kernel,
        grid=(b,),
        in_specs=[
            _per_batch(s), _per_batch(s), _per_batch(s),
            _resident((None, n_tab, LANES), lambda i: (layer, 0, 0)),
        ],
        out_specs=_per_batch(s),
        out_shape=jax.ShapeDtypeStruct((b, s, bw), BF16),
        scratch_shapes=[
            pltpu.VMEM((N_HEADS * (2 * NA_WIN_ROWS - 1) * GRID_W, LANES), F32),
            pltpu.VMEM((NA_WIN_ROWS, N_HEADS * GRID_W, NA_WIN_ROWS * GRID_W), F32),
        ],
        compiler_params=_cparams(1),
        name="neighbourhood",
    )(q, k, v, rpb_tab)


def _na_rpb_table(rpb):
    n_layers, n_heads, n_dr, n_dc = rpb.shape
    t = jnp.pad(rpb.astype(F32), ((0, 0), (0, 0), (0, 0), (0, GRID_W - n_dc)))
    t = jnp.roll(t, -(NA_WIN_COLS - 1), axis=-1)
    t = jnp.tile(t, (1, 1, 1, LANES // GRID_W)).reshape(n_layers, n_heads * n_dr, LANES)
    return jnp.pad(t, ((0, 0), (0, -(n_heads * n_dr) % 8), (0, 0)))


def _memkv_kernel(m_ref, g_ref, w_ref, gk_ref, k_ref, v_ref):
    n_batch = k_ref.shape[0]
    n_mem = k_ref.shape[1] // N_HEADS
    bw = BRANCH_WIDTH
    head_v = _head_id((1, bw), 1)
    ones_bd = _head_block_ones()
    parts = [slice(i * n_mem, (i + 1) * n_mem) for i in range(n_batch)]
    mns = [_rms_rows(m_ref[r, :], g_ref[...]).astype(BF16) for r in parts]
    kvs = [jnp.dot(mn, w_ref[...], preferred_element_type=F32) for mn in mns]
    for i, kv in enumerate(kvs):
        kn = _head_rms(kv[:, :bw], gk_ref[...], ones_bd)
        k_ref[i] = _stack_heads(kn.astype(k_ref.dtype), head_v)
        v_ref[i] = _stack_heads(kv[:, bw:].astype(v_ref.dtype), head_v)


def _memkv(mem, norm_g, w_kv, gk):
    b, m, d = mem.shape
    n_layers = w_kv.shape[0]
    bw = BRANCH_WIDTH
    out = jax.ShapeDtypeStruct((n_layers, b, N_HEADS * m, bw), BF16)
    return pl.pallas_call(
        _memkv_kernel,
        grid=(n_layers,),
        in_specs=[
            _resident((b * m, d), lambda l: (0, 0)),
            pl.BlockSpec((None, 1, d), lambda l: (l, 0, 0)),
            pl.BlockSpec((None, d, 2 * bw), lambda l: (l, 0, 0)),
            pl.BlockSpec((None, 1, bw), lambda l: (l, 0, 0)),
        ],
        out_specs=[pl.BlockSpec((None, b, N_HEADS * m, bw), lambda l: (l, 0, 0, 0))] * 2,
        out_shape=[out, out],
        compiler_params=_cparams(1),
        name="memkv",
    )(mem.reshape(b * m, d), norm_g, w_kv, gk)


def _memattn_kernel(q_ref, k_ref, v_ref, o_ref):
    n_mem = k_ref.shape[0] // N_HEADS
    sub = MEM_SUB_ROWS
    head_v = _head_id((1, BRANCH_WIDTH), 1)

    def group_body(step, carry):
        tiles = [pl.ds(pl.multiple_of((step * MEM_GROUP + g) * sub, sub), sub) for g in range(MEM_GROUP)]
        scores = [_dot_nt(q_ref[t, :], k_ref[...]) for t in tiles]
        probs, inv_ls = [], []
        for sc in scores:
            es = []
            inv_l = jnp.zeros((sub, BRANCH_WIDTH), F32)
            for h in range(N_HEADS):
                seg = sc[:, h * n_mem:(h + 1) * n_mem]
                e = jnp.exp(seg - jnp.max(seg, axis=-1, keepdims=True))
                inv_l = jnp.where(head_v == h, 1.0 / jnp.sum(e, axis=-1, keepdims=True), inv_l)
                es.append(e.astype(BF16))
            probs.append(jnp.concatenate(es, axis=1))
            inv_ls.append(inv_l)
        pvs = [jnp.dot(p, v_ref[...], preferred_element_type=F32) for p in probs]
        for t, pv, inv_l in zip(tiles, pvs, inv_ls):
            o_ref[t, :] = (pv * inv_l).astype(o_ref.dtype)
        return carry

    lax.fori_loop(0, q_ref.shape[0] // (sub * MEM_GROUP), group_body, 0)


def _memattn(q, mk, mv, layer):
    b, s, bw = q.shape
    m = mk.shape[2]
    return pl.pallas_call(
        _memattn_kernel,
        grid=(b,),
        in_specs=[
            _per_batch(s),
            pl.BlockSpec((None, None, m, bw), lambda i: (layer, i, 0, 0)),
            pl.BlockSpec((None, None, m, bw), lambda i: (layer, i, 0, 0)),
        ],
        out_specs=_per_batch(s),
        out_shape=jax.ShapeDtypeStruct((b, s, bw), BF16),
        compiler_params=_cparams(1),
        name="memattn",
    )(q, mk, mv)


def _merge_kernel(x_ref, g_ref, b0_ref, b1_ref, b2_ref, b3_ref, wg_ref, wb_ref, wo_ref, o_ref):
    d = x_ref.shape[-1]
    for halves in _row_pairs(x_ref.shape[0]):
        hs = [_rms_rows(x_ref[r, :], g_ref[...]).astype(BF16) for r in halves]
        merged = [None] * len(halves)
        for n, br_ref in enumerate((b0_ref, b1_ref, b2_ref, b3_ref)):
            for i, r in enumerate(halves):
                gate = jax.nn.sigmoid(jnp.dot(hs[i], wg_ref[:, n * d:(n + 1) * d], preferred_element_type=F32))
                up = jnp.dot(br_ref[r, :], wb_ref[n * BRANCH_WIDTH:(n + 1) * BRANCH_WIDTH, :],
                             preferred_element_type=F32)
                merged[i] = gate * up if merged[i] is None else merged[i] + gate * up
        outs = [jnp.dot(m.astype(BF16), wo_ref[...], preferred_element_type=F32) for m in merged]
        for r, t in zip(halves, outs):
            o_ref[r, :] = x_ref[r, :] + t


def _merge(x2, norm_g, branches, w_gate, w_branch, w_out, layer, tm):
    t, d = x2.shape
    bw = BRANCH_WIDTH
    br_spec = pl.BlockSpec((tm, bw), lambda i: (i, 0))
    return pl.pallas_call(
        _merge_kernel,
        grid=(t // tm,),
        in_specs=[
            pl.BlockSpec((tm, d), lambda i: (i, 0)),
            _resident((None, 1, d), lambda i: (layer, 0, 0)),
            br_spec, br_spec, br_spec, br_spec,
            _resident((d, N_BRANCH * d), lambda i: (0, 0)),
            _resident((N_BRANCH * bw, d), lambda i: (0, 0)),
            _resident((d, d), lambda i: (0, 0)),
        ],
        out_specs=pl.BlockSpec((tm, d), lambda i: (i, 0)),
        out_shape=jax.ShapeDtypeStruct((t, d), F32),
        compiler_params=_cparams(1),
        name="merge",
    )(x2, norm_g, *branches, w_gate, w_branch, w_out)


def _ffn_kernel(*refs, n_cast):
    x_ref, g_ref, w1_ref, w2_ref = refs[:4]
    o_ref = refs[4 + n_cast]
    _cast_resident_slabs(refs[4:4 + n_cast], refs[5 + n_cast:])
    ff = w2_ref.shape[0]
    for halves in _row_pairs(x_ref.shape[0]):
        hn = [_rms_rows(x_ref[r, :], g_ref[...]).astype(BF16) for r in halves]
        ag = [jnp.dot(h, w1_ref[...], preferred_element_type=F32) for h in hn]
        u = [(t[:, :ff] * jax.nn.sigmoid(t[:, :ff]) * t[:, ff:]).astype(BF16) for t in ag]
        down = [jnp.dot(t, w2_ref[...], preferred_element_type=F32) for t in u]
        for r, t in zip(halves, down):
            o_ref[r, :] = x_ref[r, :] + t


def _ffn(x2, norm_g, w1, w2, layer, tm, cast_weights=()):
    t, d = x2.shape
    ff = w2.shape[0]
    src_specs, dst_specs, cast_shapes = _cast_specs(cast_weights, layer + 1, t // tm)
    outs = pl.pallas_call(
        functools.partial(_ffn_kernel, n_cast=len(cast_weights)),
        grid=(t // tm,),
        in_specs=[
            pl.BlockSpec((tm, d), lambda i: (i, 0)),
            _resident((None, 1, d), lambda i: (layer, 0, 0)),
            _resident((d, 2 * ff), lambda i: (0, 0)),
            _resident((ff, d), lambda i: (0, 0)),
            *src_specs,
        ],
        out_specs=[pl.BlockSpec((tm, d), lambda i: (i, 0)), *dst_specs],
        out_shape=[jax.ShapeDtypeStruct((t, d), F32), *cast_shapes],
        compiler_params=_cparams(1),
        name="ffn",
    )(x2, norm_g, w1, w2, *cast_weights)
    return outs[0], list(outs[1:])


def _block_diag(w):
    n_layers, g, c, e = w.shape
    eye = jnp.eye(g, dtype=w.dtype)
    return jnp.einsum('lgce,gk->lgcke', w, eye).reshape(n_layers, g * c, g * e)


def kernel(x, mem, norm_mix_g, norm_mem_g, w_in, w_gate, ret_decay_fwd, ret_decay_bwd, ret_norm_g, pool_w,
           pool_scale, na_q_norm_g, na_k_norm_g, na_rpb, mem_q_norm_g, mem_k_norm_g, w_mem_kv, w_branch, w_out,
           norm_ffn_g, w_ffn_in, w_ffn_out):
    b, s, d = x.shape
    n_layers = w_in.shape[0]
    bw = BRANCH_WIDTH

    dense_f32 = [w_in, w_gate, w_branch.reshape(n_layers, N_BRANCH * bw, d), w_out, w_ffn_in, w_ffn_out]

    w_in_b = w_in[0].astype(BF16)
    w_mem_kv_b = w_mem_kv.astype(BF16)
    pool_w_bd = _block_diag(pool_w).astype(BF16)

    row3 = lambda a: a.astype(F32).reshape(n_layers, 1, -1)
    tile_heads = lambda g: jnp.tile(g.astype(F32), (1, N_HEADS)).reshape(n_layers, 1, bw)
    norm_mix_g3, norm_mem_g3, norm_ffn_g3 = row3(norm_mix_g), row3(norm_mem_g), row3(norm_ffn_g)
    ret_norm_g3, pool_scale3 = row3(ret_norm_g), row3(pool_scale)
    na_gq, na_gk = tile_heads(na_q_norm_g), tile_heads(na_k_norm_g)
    mem_gq, mem_gk = tile_heads(mem_q_norm_g), tile_heads(mem_k_norm_g)
    dec = jnp.concatenate([ret_decay_fwd, ret_decay_bwd], axis=1).astype(F32)
    dec = jnp.broadcast_to(dec[:, :, None], (n_layers, 2 * N_HEADS, LANES))
    rpb_tab = _na_rpb_table(na_rpb)

    half = HEAD_DIM // 2
    inv = ROPE_THETA ** (-jnp.arange(half, dtype=F32) / half)
    ang = jnp.arange(s, dtype=F32)[:, None] * inv[None, :]
    cos = jnp.tile(jnp.concatenate([jnp.cos(ang), jnp.cos(ang)], axis=1), (1, N_HEADS))
    sin = jnp.tile(jnp.concatenate([-jnp.sin(ang), jnp.sin(ang)], axis=1), (1, N_HEADS))

    mk_all, mv_all = _memkv(mem, norm_mem_g3, w_mem_kv_b, mem_gk)

    x2 = x.reshape(b * s, d)
    for layer in range(n_layers):
        proj, dense_rest_b = _inproj(x2, norm_mix_g3, w_in_b, cos, sin, na_gq, na_gk, mem_gq, layer,
                                     tm=INPROJ_TILE_ROWS, cast_weights=dense_f32[1:])
        rq, rk, rv, rg, pv, nq, nk, nv, mq = [t.reshape(b, s, bw) for t in proj]
        w_gate_b, w_branch_b, w_out_b, w_ffn_in_b, w_ffn_out_b = dense_rest_b
        ret = _retention(rq, rk, rv, rg, dec, ret_norm_g3, layer)
        pool = _pooling(pv, pool_w_bd, pool_scale3, layer)
        na = _neighbourhood(nq, nk, nv, rpb_tab, layer)
        mo = _memattn(mq, mk_all, mv_all, layer)
        branches = [t.reshape(b * s, bw) for t in (ret, pool, na, mo)]
        x2 = _merge(x2, norm_mix_g3, branches, w_gate_b, w_branch_b, w_out_b, layer, tm=MERGE_TILE_ROWS)
        later = dense_f32[:1] if layer + 1 < n_layers else []
        x2, cast_next = _ffn(x2, norm_ffn_g3, w_ffn_in_b, w_ffn_out_b, layer, tm=FFN_TILE_ROWS, cast_weights=later)
        if cast_next:
            w_in_b, = cast_next
    return x2.reshape(b, s, d)
```

```python
import functools

import jax
import jax.numpy as jnp
from jax import lax
from jax.experimental import pallas as pl
from jax.experimental.pallas import tpu as pltpu

F32 = jnp.float32
BF16 = jnp.bfloat16

HEAD_DIM = 64
HEAD_DIM_LOG2 = HEAD_DIM.bit_length() - 1
BRANCH_WIDTH = 256
N_HEADS = BRANCH_WIDTH // HEAD_DIM
N_BRANCH = 4
GRID_W = 64
RET_CHUNK = 128
RET_GROUP = 16
NA_GROUP = 32
MEM_SUB_ROWS = 128
MEM_GROUP = 16
ROPE_THETA = 10000.0
POOL_WINDOWS = (2, 4, 8, 16)
POOL_PAD = 16
NA_WIN_ROWS = 8
NA_WIN_COLS = 16
NEG_INF = -1e30
EPS = 1e-6
QK_SCALE = HEAD_DIM ** -0.5

LANES = 128
VMEM_LIMIT_BYTES = 56 * 1024 * 1024

INPROJ_TILE_ROWS = 1024
MERGE_TILE_ROWS = 1024
DENSE_PART_ROWS = 256
FFN_TILE_ROWS = 1024


def _cparams(n_grid_dims):
    return pltpu.CompilerParams(
        dimension_semantics=("arbitrary",) * n_grid_dims,
        vmem_limit_bytes=VMEM_LIMIT_BYTES,
    )


def _resident(block_shape, index_map):
    return pl.BlockSpec(block_shape, index_map, pipeline_mode=pl.Buffered(1))


def _rms_rows(x, g):
    ms = jnp.mean(x * x, axis=-1, keepdims=True)
    return x * lax.rsqrt(ms + EPS) * g


def _head_id(shape, axis):
    return lax.shift_right_logical(lax.broadcasted_iota(jnp.int32, shape, axis), HEAD_DIM_LOG2)


def _head_block_ones():
    shape = (BRANCH_WIDTH, BRANCH_WIDTH)
    return (_head_id(shape, 0) == _head_id(shape, 1)).astype(BF16)


def _head_sumsq(t, ones_bd):
    return jnp.dot((t * t).astype(BF16), ones_bd, preferred_element_type=F32)


def _head_rms(t, g, ones_bd):
    ms = _head_sumsq(t, ones_bd) * (1.0 / HEAD_DIM)
    return t * lax.rsqrt(ms + EPS) * g


def _row_halves(n):
    return [slice(0, n // 2), slice(n // 2, n)]


def _row_pairs(n):
    parts = [slice(i, i + DENSE_PART_ROWS) for i in range(0, n, DENSE_PART_ROWS)]
    return [parts[i:i + 2] for i in range(0, len(parts), 2)]


def _log_sigmoid(x):
    return jnp.minimum(x, 0.0) - jnp.log1p(jnp.exp(-jnp.abs(x)))


def _dot_nt(a, b):
    return lax.dot_general(a, b, (((1,), (1,)), ((), ())), preferred_element_type=F32)


def _dot_tn(a, b):
    return lax.dot_general(a, b, (((0,), (0,)), ((), ())), preferred_element_type=F32)


BF16_SUBLANES = 16


def _cast_specs(cast_weights, cast_layer, n_steps):
    src_specs, dst_specs, out_shapes = [], [], []
    for w in cast_weights:
        assert w.shape[1] % (n_steps * BF16_SUBLANES) == 0, (w.shape, n_steps)
        rows = w.shape[1] // n_steps
        src_specs.append(pl.BlockSpec((None, rows, w.shape[2]), lambda i: (cast_layer, i, 0)))
        dst_specs.append(pl.BlockSpec((rows, w.shape[2]), lambda i: (i, 0)))
        out_shapes.append(jax.ShapeDtypeStruct(w.shape[1:], BF16))
    return src_specs, dst_specs, out_shapes


def _cast_resident_slabs(cast_src, cast_dst):
    for src, dst in zip(cast_src, cast_dst):
        dst[...] = src[...].astype(dst.dtype)


def _swap_rotary_halves(t):
    half = HEAD_DIM // 2
    lane = lax.broadcasted_iota(jnp.int32, (1, LANES), 1)
    in_first_half = jnp.bitwise_and(lane, HEAD_DIM - 1) < half
    blocks = []
    for c0 in range(0, t.shape[1], LANES):
        tb = t[:, c0:c0 + LANES]
        blocks.append(jnp.where(in_first_half, pltpu.roll(tb, LANES - half, 1), pltpu.roll(tb, half, 1)))
    return jnp.concatenate(blocks, axis=1)


def _inproj_kernel(*refs, n_cast):
    x_ref, g_ref, w_ref, cos_ref, sin_ref, gnq_ref, gnk_ref, gmq_ref = refs[:8]
    n_proj = w_ref.shape[1] // BRANCH_WIDTH
    o_refs = refs[8 + n_cast:8 + n_cast + n_proj]
    _cast_resident_slabs(refs[8:8 + n_cast], refs[8 + n_cast + n_proj:])
    bw = BRANCH_WIDTH
    ones_bd = _head_block_ones()
    halves = _row_halves(x_ref.shape[0])
    hs = [_rms_rows(x_ref[r, :], g_ref[...]).astype(BF16) for r in halves]
    ps = [jnp.dot(h, w_ref[...], preferred_element_type=F32) for h in hs]
    for r, p in zip(halves, ps):
        cos, sin = cos_ref[r, :], sin_ref[r, :]

        def store(c0, val):
            for c in range(0, val.shape[1], bw):
                o_ref = o_refs[(c0 + c) // bw]
                o_ref[r, :] = val[:, c:c + bw].astype(o_ref.dtype)

        for c0, scale in ((0, QK_SCALE), (bw, None)):
            t = p[:, c0:c0 + bw]
            rot = t * cos + _swap_rotary_halves(t) * sin
            store(c0, rot if scale is None else rot * scale)
        store(2 * bw, p[:, 2 * bw:5 * bw])
        store(5 * bw, _head_rms(p[:, 5 * bw:6 * bw], gnq_ref[...], ones_bd) * QK_SCALE)
        store(6 * bw, _head_rms(p[:, 6 * bw:7 * bw], gnk_ref[...], ones_bd))
        store(7 * bw, p[:, 7 * bw:8 * bw])
        store(8 * bw, _head_rms(p[:, 8 * bw:9 * bw], gmq_ref[...], ones_bd) * QK_SCALE)


def _inproj(x2, norm_g, w_in, cos, sin, na_gq, na_gk, mem_gq, layer, tm, cast_weights=()):
    t, d = x2.shape
    n_out = w_in.shape[-1]
    bw = BRANCH_WIDTH
    n_proj = n_out // bw
    seq_tiles = cos.shape[0] // tm
    head_gain = _resident((None, 1, bw), lambda i: (layer, 0, 0))
    src_specs, dst_specs, cast_shapes = _cast_specs(cast_weights, layer, t // tm)
    outs = pl.pallas_call(
        functools.partial(_inproj_kernel, n_cast=len(cast_weights)),
        grid=(t // tm,),
        in_specs=[
            pl.BlockSpec((tm, d), lambda i: (i, 0)),
            _resident((None, 1, d), lambda i: (layer, 0, 0)),
            _resident((d, n_out), lambda i: (0, 0)),
            pl.BlockSpec((tm, bw), lambda i: (i % seq_tiles, 0)),
            pl.BlockSpec((tm, bw), lambda i: (i % seq_tiles, 0)),
            head_gain, head_gain, head_gain,
            *src_specs,
        ],
        out_specs=[*[pl.BlockSpec((tm, bw), lambda i: (i, 0))] * n_proj, *dst_specs],
        out_shape=[*[jax.ShapeDtypeStruct((t, bw), BF16)] * n_proj, *cast_shapes],
        compiler_params=_cparams(1),
        name="inproj",
    )(x2, norm_g, w_in, cos, sin, na_gq, na_gk, mem_gq, *cast_weights)
    return list(outs[:n_proj]), list(outs[n_proj:])


def _ret_kernel(q_ref, k_ref, v_ref, g_ref, dec_ref, gn_ref, o_ref, sf_ref, sb_ref, st_ref, dm_ref):
    c = RET_CHUNK
    n_chunks = q_ref.shape[0] // c
    head = _head_id((1, BRANCH_WIDTH), 1)

    lg = _log_sigmoid(dec_ref[...])
    lg2 = jnp.concatenate([lg, lg], axis=1)

    def per_lane(row0):
        out = jnp.zeros((1, BRANCH_WIDTH), F32)
        for h in range(N_HEADS):
            out = jnp.where(head == h, lg2[row0 + h:row0 + h + 1, :], out)
        return out

    lgf, lgb = per_lane(0), per_lane(N_HEADS)

    idx = lax.broadcasted_iota(jnp.int32, (c, 1), 0).astype(F32)
    kdec_f = jnp.exp((c - 1 - idx) * lgf)
    qdec_f = jnp.exp((idx + 1) * lgf)
    kdec_b = jnp.exp(idx * lgb)
    qdec_b = jnp.exp((c - idx) * lgb)
    cdec_f = jnp.exp(c * lgf)
    cdec_b = jnp.exp(c * lgb)

    diff = (lax.broadcasted_iota(jnp.int32, (c, c), 0)
            - lax.broadcasted_iota(jnp.int32, (c, c), 1)).astype(F32)
    for h in range(N_HEADS):
        lf = lg[h:h + 1, :]
        lb = lg[N_HEADS + h:N_HEADS + h + 1, :]
        dm_ref[:, h * c:(h + 1) * c] = jnp.where(diff >= 0, jnp.exp(jnp.maximum(diff, 0.0) * lf),
                                                 jnp.exp(jnp.maximum(-diff, 0.0) * lb))

    state_shape = (BRANCH_WIDTH, BRANCH_WIDTH)
    same_head = _head_id(state_shape, 0) == _head_id(state_shape, 1)

    group = RET_GROUP
    chunk_rows = lambda j: pl.ds(pl.multiple_of(j * c, c), c)

    def sweep(kdec, cdec, states_ref, chunk_of):
        st_ref[...] = jnp.zeros_like(st_ref)

        def body(step, carry):
            chunks = [chunk_of(step * group + g) for g in range(group)]
            kvs = [_dot_tn((k_ref[chunk_rows(j), :].astype(F32) * kdec).astype(BF16), v_ref[chunk_rows(j), :])
                   for j in chunks]
            state = st_ref[...]
            for j, kv in zip(chunks, kvs):
                states_ref[j] = state.astype(BF16)
                state = cdec * state + jnp.where(same_head, kv, 0.0)
            st_ref[...] = state
            return carry

        lax.fori_loop(0, n_chunks // group, body, 0)

    sweep(kdec_f, cdec_f, sf_ref, lambda i: i)
    sweep(kdec_b, cdec_b, sb_ref, lambda i: n_chunks - 1 - i)

    ones_bd = _head_block_ones()
    gn = gn_ref[...]

    def out_body(step, carry):
        chunks = [step * group + g for g in range(group)]
        accs = []
        for j in chunks:
            rows = chunk_rows(j)
            qb = q_ref[rows, :]
            q = qb.astype(F32)
            cross = (jnp.dot((q * qdec_f).astype(BF16), sf_ref[j], preferred_element_type=F32)
                     + jnp.dot((q * qdec_b).astype(BF16), sb_ref[j], preferred_element_type=F32))
            scores = _dot_nt(qb, _stack_heads(k_ref[rows, :], head))
            accs.append((cross, scores))
        accs = [cross + jnp.dot((scores * dm_ref[...]).astype(BF16), _stack_heads(v_ref[chunk_rows(j), :], head),
                                preferred_element_type=F32)
                for j, (cross, scores) in zip(chunks, accs)]
        sumsq = [_head_sumsq(acc, ones_bd) for acc in accs]
        for j, acc, ss in zip(chunks, accs, sumsq):
            rows = chunk_rows(j)
            y = acc * lax.rsqrt(ss * (1.0 / HEAD_DIM) + EPS) * gn
            gate = g_ref[rows, :].astype(F32)
            o_ref[rows, :] = (y * (gate * jax.nn.sigmoid(gate))).astype(o_ref.dtype)
        return carry

    lax.fori_loop(0, n_chunks // group, out_body, 0)


def _per_batch(s):
    return pl.BlockSpec((None, s, BRANCH_WIDTH), lambda i: (i, 0, 0))


def _retention(q, k, v, g, dec, gn, layer):
    b, s, bw = q.shape
    n_chunks = s // RET_CHUNK
    return pl.pallas_call(
        _ret_kernel,
        grid=(b,),
        in_specs=[
            _per_batch(s), _per_batch(s), _per_batch(s), _per_batch(s),
            _resident((None, 2 * N_HEADS, LANES), lambda i: (layer, 0, 0)),
            _resident((None, 1, bw), lambda i: (layer, 0, 0)),
        ],
        out_specs=_per_batch(s),
        out_shape=jax.ShapeDtypeStruct((b, s, bw), BF16),
        scratch_shapes=[
            pltpu.VMEM((n_chunks, bw, bw), BF16),
            pltpu.VMEM((n_chunks, bw, bw), BF16),
            pltpu.VMEM((bw, bw), F32),
            pltpu.VMEM((RET_CHUNK, N_HEADS * RET_CHUNK), F32),
        ],
        compiler_params=_cparams(1),
        name="retention",
    )(q, k, v, g, dec, gn)


def _pool_kernel(v_ref, w_ref, sc_ref, o_ref, pad_ref):
    s = v_ref.shape[0]
    pad = POOL_PAD
    tile = 512
    ext = tile + 2 * pad

    pad_ref[0:pad, :] = jnp.zeros((pad, BRANCH_WIDTH), F32)
    pad_ref[pad + s:pad + s + pad, :] = jnp.zeros((pad, BRANCH_WIDTH), F32)
    pad_ref[pad:pad + s, :] = v_ref[...].astype(F32)

    grp = _head_id((1, BRANCH_WIDTH), 1)
    halfw = jnp.where(grp == 0, POOL_WINDOWS[0] // 2,
                      jnp.where(grp == 1, POOL_WINDOWS[1] // 2,
                                jnp.where(grp == 2, POOL_WINDOWS[2] // 2, POOL_WINDOWS[3] // 2)))

    half_lanes = BRANCH_WIDTH // 2
    grp_lo, grp_hi = grp[:, :half_lanes], grp[:, half_lanes:]
    for ci in range(s // tile):
        p = pad_ref[ci * tile:ci * tile + ext, :]
        s2 = p + pltpu.roll(p, 1, 0)
        s4 = pltpu.roll(s2, 1, 0) + pltpu.roll(s2, ext - 1, 0)
        s4_hi = s4[:, half_lanes:]
        s8 = pltpu.roll(s4_hi, 2, 0) + pltpu.roll(s4_hi, ext - 2, 0)
        s16 = pltpu.roll(s8, 4, 0) + pltpu.roll(s8, ext - 4, 0)
        win = jnp.concatenate([jnp.where(grp_lo == 0, s2[:, :half_lanes], s4[:, :half_lanes]),
                               jnp.where(grp_hi == 2, s8, s16)], axis=1)
        win = win[pad:pad + tile, :]
        if 0 < ci < s // tile - 1:
            pooled = win * (0.5 / halfw.astype(F32)) - p[pad:pad + tile, :]
        else:
            t = ci * tile + lax.broadcasted_iota(jnp.int32, (tile, 1), 0)
            count = (jnp.minimum(t + halfw, s) - jnp.maximum(t - halfw, 0)).astype(F32)
            pooled = win / count - p[pad:pad + tile, :]
        mixed = jnp.dot(pooled.astype(BF16), w_ref[...], preferred_element_type=F32) * sc_ref[...]
        o_ref[ci * tile:(ci + 1) * tile, :] = mixed.astype(o_ref.dtype)


def _pooling(v, w_bd, scale, layer):
    b, s, bw = v.shape
    return pl.pallas_call(
        _pool_kernel,
        grid=(b,),
        in_specs=[
            _per_batch(s),
            _resident((None, bw, bw), lambda i: (layer, 0, 0)),
            _resident((None, 1, bw), lambda i: (layer, 0, 0)),
        ],
        out_specs=_per_batch(s),
        out_shape=jax.ShapeDtypeStruct((b, s, bw), BF16),
        scratch_shapes=[pltpu.VMEM((s + 2 * POOL_PAD, bw), F32)],
        compiler_params=_cparams(1),
        name="pooling",
    )(v, w_bd, scale)


def _stack_heads(t, head_vec):
    return jnp.concatenate([jnp.where(head_vec == h, t, jnp.zeros_like(t)) for h in range(N_HEADS)], axis=0)


def _unstack_heads(t, head_vec, n):
    out = jnp.where(head_vec == 0, t[0:n], 0.0)
    for h in range(1, N_HEADS):
        out = out + jnp.where(head_vec == h, t[h * n:(h + 1) * n], 0.0)
    return out


def _na_pattern_first_offset(pat, rows):
    wr = NA_WIN_ROWS
    query_row = pat if pat <= wr // 2 else rows - wr + pat
    first_key_row = min(max(query_row - wr // 2, 0), rows - wr)
    return first_key_row - query_row + wr - 1


def _na_build_bias(tab_ref, toep_ref, bias_ref, rows):
    wr, wc = NA_WIN_ROWS, NA_WIN_COLS
    n_rel = N_HEADS * (2 * wr - 1)
    x = jnp.concatenate([jnp.broadcast_to(tab_ref[i:i + 1, :], (GRID_W, LANES)) for i in range(n_rel)], axis=0)
    qcol = jnp.bitwise_and(lax.broadcasted_iota(jnp.int32, x.shape, 0), GRID_W - 1)
    for bit in range(GRID_W.bit_length() - 1):
        x = jnp.where(jnp.bitwise_and(qcol, 1 << bit) != 0, pltpu.roll(x, 1 << bit, 1), x)
    toep_ref[...] = x

    lane = lax.broadcasted_iota(jnp.int32, (GRID_W, LANES), 1)
    kcol = jnp.bitwise_and(lane, GRID_W - 1)
    qwin = jnp.clip(lax.broadcasted_iota(jnp.int32, (GRID_W, LANES), 0) - wc // 2, 0, GRID_W - wc)
    in_window = (kcol >= qwin) & (kcol < qwin + wc)
    first_half = lane < GRID_W
    for pat in range(wr):
        dr0 = _na_pattern_first_offset(pat, rows)
        for h in range(N_HEADS):
            for pair in range(wr // 2):
                i0 = h * (2 * wr - 1) + dr0 + 2 * pair
                t = jnp.where(first_half, toep_ref[i0 * GRID_W:(i0 + 1) * GRID_W, :],
                              toep_ref[(i0 + 1) * GRID_W:(i0 + 2) * GRID_W, :])
                bias_ref[pat, h * GRID_W:(h + 1) * GRID_W, pair * LANES:(pair + 1) * LANES] = (
                    jnp.where(in_window, t, NEG_INF))


def _na_kernel(q_ref, k_ref, v_ref, tab_ref, o_ref, toep_ref, bias_ref):
    s = q_ref.shape[0]
    rows = s // GRID_W
    wr = NA_WIN_ROWS
    head_v = _head_id((1, BRANCH_WIDTH), 1)

    @pl.when(pl.program_id(0) == 0)
    def _():
        _na_build_bias(tab_ref, toep_ref, bias_ref, rows)

    def group_body(step, carry):
        slices, scores = [], []
        for g in range(NA_GROUP):
            r = step * NA_GROUP + g
            r0 = jnp.clip(r - wr // 2, 0, rows - wr)
            pat = jnp.where(r < wr // 2, r, jnp.where(r > rows - wr // 2, r - (rows - wr), wr // 2))
            qrows = pl.ds(pl.multiple_of(r * GRID_W, GRID_W), GRID_W)
            krows = pl.ds(pl.multiple_of(r0 * GRID_W, GRID_W), wr * GRID_W)
            qst = _stack_heads(q_ref[qrows, :], head_v)
            scores.append(_dot_nt(qst, k_ref[krows, :]) + bias_ref[pat])
            slices.append((qrows, krows))
        probs = []
        for sc in scores:
            e = jnp.exp(sc - jnp.max(sc, axis=-1, keepdims=True))
            probs.append((e.astype(BF16), 1.0 / jnp.sum(e, axis=-1, keepdims=True)))
        pvs = [jnp.dot(e, v_ref[krows, :], preferred_element_type=F32) * inv_l
               for (e, inv_l), (_, krows) in zip(probs, slices)]
        for pv, (qrows, _) in zip(pvs, slices):
            o_ref[qrows, :] = _unstack_heads(pv, head_v, GRID_W).astype(o_ref.dtype)
        return carry

    lax.fori_loop(0, rows // NA_GROUP, group_body, 0)


def _neighbourhood(q, k, v, rpb_tab, layer):
    b, s, bw = q.shape
    n_tab = rpb_tab.shape[1]
    return pl.pallas_call(
        _na_kernel,
        grid=(b,),
        in_specs=[
            _per_batch(s), _per_batch(s), _per_batch(s),
            _resident((None, n_tab, LANES), lambda i: (layer, 0, 0)),
        ],
        out_specs=_per_batch(s),
        out_shape=jax.ShapeDtypeStruct((b, s, bw), BF16),
        scratch_shapes=[
            pltpu.VMEM((N_HEADS * (2 * NA_WIN_ROWS - 1) * GRID_W, LANES), F32),
            pltpu.VMEM((NA_WIN_ROWS, N_HEADS * GRID_W, NA_WIN_ROWS * GRID_W), F32),
        ],
        compiler_params=_cparams(1),
        name="neighbourhood",
    )(q, k, v, rpb_tab)


def _na_rpb_table(rpb):
    n_layers, n_heads, n_dr, n_dc = rpb.shape
    t = jnp.pad(rpb.astype(F32), ((0, 0), (0, 0), (0, 0), (0, GRID_W - n_dc)))
    t = jnp.roll(t, -(NA_WIN_COLS - 1), axis=-1)
    t = jnp.tile(t, (1, 1, 1, LANES // GRID_W)).reshape(n_layers, n_heads * n_dr, LANES)
    return jnp.pad(t, ((0, 0), (0, -(n_heads * n_dr) % 8), (0, 0)))


def _memkv_kernel(m_ref, g_ref, w_ref, gk_ref, k_ref, v_ref):
    n_batch = k_ref.shape[0]
    n_mem = k_ref.shape[1] // N_HEADS
    bw = BRANCH_WIDTH
    head_v = _head_id((1, bw), 1)
    ones_bd = _head_block_ones()
    parts = [slice(i * n_mem, (i + 1) * n_mem) for i in range(n_batch)]
    mns = [_rms_rows(m_ref[r, :], g_ref[...]).astype(BF16) for r in parts]
    kvs = [jnp.dot(mn, w_ref[...], preferred_element_type=F32) for mn in mns]
    for i, kv in enumerate(kvs):
        kn = _head_rms(kv[:, :bw], gk_ref[...], ones_bd)
        k_ref[i] = _stack_heads(kn.astype(k_ref.dtype), head_v)
        v_ref[i] = _stack_heads(kv[:, bw:].astype(v_ref.dtype), head_v)


def _memkv(mem, norm_g, w_kv, gk):
    b, m, d = mem.shape
    n_layers = w_kv.shape[0]
    bw = BRANCH_WIDTH
    out = jax.ShapeDtypeStruct((n_layers, b, N_HEADS * m, bw), BF16)
    return pl.pallas_call(
        _memkv_kernel,
        grid=(n_layers,),
        in_specs=[
            _resident((b * m, d), lambda l: (0, 0)),
            pl.BlockSpec((None, 1, d), lambda l: (l, 0, 0)),
            pl.BlockSpec((None, d, 2 * bw), lambda l: (l, 0, 0)),
            pl.BlockSpec((None, 1, bw), lambda l: (l, 0, 0)),
        ],
        out_specs=[pl.BlockSpec((None, b, N_HEADS * m, bw), lambda l: (l, 0, 0, 0))] * 2,
        out_shape=[out, out],
        compiler_params=_cparams(1),
        name="memkv",
    )(mem.reshape(b * m, d), norm_g, w_kv, gk)


def _memattn_kernel(q_ref, k_ref, v_ref, o_ref):
    n_mem = k_ref.shape[0] // N_HEADS
    sub = MEM_SUB_ROWS
    head_v = _head_id((1, BRANCH_WIDTH), 1)

    def group_body(step, carry):
        tiles = [pl.ds(pl.multiple_of((step * MEM_GROUP + g) * sub, sub), sub) for g in range(MEM_GROUP)]
        scores = [_dot_nt(q_ref[t, :], k_ref[...]) for t in tiles]
        probs, inv_ls = [], []
        for sc in scores:
            es = []
            inv_l = jnp.zeros((sub, BRANCH_WIDTH), F32)
            for h in range(N_HEADS):
                seg = sc[:, h * n_mem:(h + 1) * n_mem]
                e = jnp.exp(seg - jnp.max(seg, axis=-1, keepdims=True))
                inv_l = jnp.where(head_v == h, 1.0 / jnp.sum(e, axis=-1, keepdims=True), inv_l)
                es.append(e.astype(BF16))
            probs.append(jnp.concatenate(es, axis=1))
            inv_ls.append(inv_l)
        pvs = [jnp.dot(p, v_ref[...], preferred_element_type=F32) for p in probs]
        for t, pv, inv_l in zip(tiles, pvs, inv_ls):
            o_ref[t, :] = (pv * inv_l).astype(o_ref.dtype)
        return carry

    lax.fori_loop(0, q_ref.shape[0] // (sub * MEM_GROUP), group_body, 0)


def _memattn(q, mk, mv, layer):
    b, s, bw = q.shape
    m = mk.shape[2]
    return pl.pallas_call(
        _memattn_kernel,
        grid=(b,),
        in_specs=[
            _per_batch(s),
            pl.BlockSpec((None, None, m, bw), lambda i: (layer, i, 0, 0)),
            pl.BlockSpec((None, None, m, bw), lambda i: (layer, i, 0, 0)),
        ],
        out_specs=_per_batch(s),
        out_shape=jax.ShapeDtypeStruct((b, s, bw), BF16),
        compiler_params=_cparams(1),
        name="memattn",
    )(q, mk, mv)


def _merge_kernel(x_ref, g_ref, b0_ref, b1_ref, b2_ref, b3_ref, wg_ref, wb_ref, wo_ref, o_ref):
    d = x_ref.shape[-1]
    for halves in _row_pairs(x_ref.shape[0]):
        hs = [_rms_rows(x_ref[r, :], g_ref[...]).astype(BF16) for r in halves]
        merged = [None] * len(halves)
        for n, br_ref in enumerate((b0_ref, b1_ref, b2_ref, b3_ref)):
            for i, r in enumerate(halves):
                gate = jax.nn.sigmoid(jnp.dot(hs[i], wg_ref[:, n * d:(n + 1) * d], preferred_element_type=F32))
                up = jnp.dot(br_ref[r, :], wb_ref[n * BRANCH_WIDTH:(n + 1) * BRANCH_WIDTH, :],
                             preferred_element_type=F32)
                merged[i] = gate * up if merged[i] is None else merged[i] + gate * up
        outs = [jnp.dot(m.astype(BF16), wo_ref[...], preferred_element_type=F32) for m in merged]
        for r, t in zip(halves, outs):
            o_ref[r, :] = x_ref[r, :] + t


def _merge(x2, norm_g, branches, w_gate, w_branch, w_out, layer, tm):
    t, d = x2.shape
    bw = BRANCH_WIDTH
    br_spec = pl.BlockSpec((tm, bw), lambda i: (i, 0))
    return pl.pallas_call(
        _merge_kernel,
        grid=(t // tm,),
        in_specs=[
            pl.BlockSpec((tm, d), lambda i: (i, 0)),
            _resident((None, 1, d), lambda i: (layer, 0, 0)),
            br_spec, br_spec, br_spec, br_spec,
            _resident((d, N_BRANCH * d), lambda i: (0, 0)),
            _resident((N_BRANCH * bw, d), lambda i: (0, 0)),
            _resident((d, d), lambda i: (0, 0)),
        ],
        out_specs=pl.BlockSpec((tm, d), lambda i: (i, 0)),
        out_shape=jax.ShapeDtypeStruct((t, d), F32),
        compiler_params=_cparams(1),
        name="merge",
    )(x2, norm_g, *branches, w_gate, w_branch, w_out)


def _ffn_kernel(*refs, n_cast):
    x_ref, g_ref, w1_ref, w2_ref = refs[:4]
    o_ref = refs[4 + n_cast]
    _cast_resident_slabs(refs[4:4 + n_cast], refs[5 + n_cast:])
    ff = w2_ref.shape[0]
    for halves in _row_pairs(x_ref.shape[0]):
        hn = [_rms_rows(x_ref[r, :], g_ref[...]).astype(BF16) for r in halves]
        ag = [jnp.dot(h, w1_ref[...], preferred_element_type=F32) for h in hn]
        u = [(t[:, :ff] * jax.nn.sigmoid(t[:, :ff]) * t[:, ff:]).astype(BF16) for t in ag]
        down = [jnp.dot(t, w2_ref[...], preferred_element_type=F32) for t in u]
        for r, t in zip(halves, down):
            o_ref[r, :] = x_ref[r, :] + t


def _ffn(x2, norm_g, w1, w2, layer, tm, cast_weights=()):
    t, d = x2.shape
    ff = w2.shape[0]
    src_specs, dst_specs, cast_shapes = _cast_specs(cast_weights, layer + 1, t // tm)
    outs = pl.pallas_call(
        functools.partial(_ffn_kernel, n_cast=len(cast_weights)),
        grid=(t // tm,),
        in_specs=[
            pl.BlockSpec((tm, d), lambda i: (i, 0)),
            _resident((None, 1, d), lambda i: (layer, 0, 0)),
            _resident((d, 2 * ff), lambda i: (0, 0)),
            _resident((ff, d), lambda i: (0, 0)),
            *src_specs,
        ],
        out_specs=[pl.BlockSpec((tm, d), lambda i: (i, 0)), *dst_specs],
        out_shape=[jax.ShapeDtypeStruct((t, d), F32), *cast_shapes],
        compiler_params=_cparams(1),
        name="ffn",
    )(x2, norm_g, w1, w2, *cast_weights)
    return outs[0], list(outs[1:])


def _block_diag(w):
    n_layers, g, c, e = w.shape
    eye = jnp.eye(g, dtype=w.dtype)
    return jnp.einsum('lgce,gk->lgcke', w, eye).reshape(n_layers, g * c, g * e)


def kernel(x, mem, norm_mix_g, norm_mem_g, w_in, w_gate, ret_decay_fwd, ret_decay_bwd, ret_norm_g, pool_w,
           pool_scale, na_q_norm_g, na_k_norm_g, na_rpb, mem_q_norm_g, mem_k_norm_g, w_mem_kv, w_branch, w_out,
           norm_ffn_g, w_ffn_in, w_ffn_out):
    b, s, d = x.shape
    n_layers = w_in.shape[0]
    bw = BRANCH_WIDTH

    dense_f32 = [w_in, w_gate, w_branch.reshape(n_layers, N_BRANCH * bw, d), w_out, w_ffn_in, w_ffn_out]

    w_in_b = w_in[0].astype(BF16)
    w_mem_kv_b = w_mem_kv.astype(BF16)
    pool_w_bd = _block_diag(pool_w).astype(BF16)

    row3 = lambda a: a.astype(F32).reshape(n_layers, 1, -1)
    tile_heads = lambda g: jnp.tile(g.astype(F32), (1, N_HEADS)).reshape(n_layers, 1, bw)
    norm_mix_g3, norm_mem_g3, norm_ffn_g3 = row3(norm_mix_g), row3(norm_mem_g), row3(norm_ffn_g)
    ret_norm_g3, pool_scale3 = row3(ret_norm_g), row3(pool_scale)
    na_gq, na_gk = tile_heads(na_q_norm_g), tile_heads(na_k_norm_g)
    mem_gq, mem_gk = tile_heads(mem_q_norm_g), tile_heads(mem_k_norm_g)
    dec = jnp.concatenate([ret_decay_fwd, ret_decay_bwd], axis=1).astype(F32)
    dec = jnp.broadcast_to(dec[:, :, None], (n_layers, 2 * N_HEADS, LANES))
    rpb_tab = _na_rpb_table(na_rpb)

    half = HEAD_DIM // 2
    inv = ROPE_THETA ** (-jnp.arange(half, dtype=F32) / half)
    ang = jnp.arange(s, dtype=F32)[:, None] * inv[None, :]
    cos = jnp.tile(jnp.concatenate([jnp.cos(ang), jnp.cos(ang)], axis=1), (1, N_HEADS))
    sin = jnp.tile(jnp.concatenate([-jnp.sin(ang), jnp.sin(ang)], axis=1), (1, N_HEADS))

    mk_all, mv_all = _memkv(mem, norm_mem_g3, w_mem_kv_b, mem_gk)

    x2 = x.reshape(b * s, d)
    for layer in range(n_layers):
        proj, dense_rest_b = _inproj(x2, norm_mix_g3, w_in_b, cos, sin, na_gq, na_gk, mem_gq, layer,
                                     tm=INPROJ_TILE_ROWS, cast_weights=dense_f32[1:])
        rq, rk, rv, rg, pv, nq, nk, nv, mq = [t.reshape(b, s, bw) for t in proj]
        w_gate_b, w_branch_b, w_out_b, w_ffn_in_b, w_ffn_out_b = dense_rest_b
        ret = _retention(rq, rk, rv, rg, dec, ret_norm_g3, layer)
        pool = _pooling(pv, pool_w_bd, pool_scale3, layer)
        na = _neighbourhood(nq, nk, nv, rpb_tab, layer)
        mo = _memattn(mq, mk_all, mv_all, layer)
        branches = [t.reshape(b * s, bw) for t in (ret, pool, na, mo)]
        x2 = _merge(x2, norm_mix_g3, branches, w_gate_b, w_branch_b, w_out_b, layer, tm=MERGE_TILE_ROWS)
        later = dense_f32[:1] if layer + 1 < n_layers else []
        x2, cast_next = _ffn(x2, norm_ffn_g3, w_ffn_in_b, w_ffn_out_b, layer, tm=FFN_TILE_ROWS, cast_weights=later)
        if cast_next:
            w_in_b, = cast_next
    return x2.reshape(b, s, d)
```
